```python
import math, functools
import jax, jax.numpy as jnp
from jax import lax
import numpy as np

D_MODEL = 1024
BATCH = 4
SEQ = 4096
DEPTH = 2
DEC_BATCH = 128
DEC_SEQ = 4
PAST_LEN = 2048
PAGE_SIZE = 128

MIX_W = D_MODEL
A_W = MIX_W // 2
B_W = MIX_W // 2
C_W = MIX_W // 2
ATT_W = MIX_W // 2
CHUNK = 128
A_GROUPS = A_W // CHUNK
A_GROUP_W = A_W // A_GROUPS
CONV_W = 3
POOL_WINDOWS = (2, 4, 8, 16)
C_GROUPS = len(POOL_WINDOWS)
C_GROUP_W = C_W // C_GROUPS
POOL_HIST = max(POOL_WINDOWS) - 1
HEAD_DIM = 64
N_HEADS = ATT_W // HEAD_DIM
IDX_HEADS = 8
IDX_DIM = 64
TOPK_MAX = 256
Q_BLOCK = 128
ROPE_THETA = 500000.0
D_FF = ((8 * D_MODEL // 3 + 127) // 128) * 128
N_EVEN = (DEPTH + 1) // 2
N_ODD = DEPTH // 2
ALPHA = (2.0 * DEPTH) ** 0.25
BETA = (8.0 * DEPTH) ** -0.25
LN_EPS = 1e-5
ATT_SCALE = HEAD_DIM ** -0.5

E_U = 0
E_V = E_U + A_W
E_B = E_V + A_W
E_C = E_B + B_W
E_H = E_C + B_W
EVEN_COLS = E_H + B_W
O_P = 0
O_Q = O_P + C_W
O_K = O_Q + ATT_W
O_V = O_K + ATT_W
O_QI = O_V + ATT_W
O_KI = O_QI + IDX_HEADS * IDX_DIM
O_WI = O_KI + IDX_DIM
ODD_COLS = O_WI + IDX_HEADS

kernel_name = 'hybrid_gmlp_conv_pool_dsa_step'


def layer_norm(x, g, b):
    xf = x.astype(jnp.float32)
    mu = jnp.mean(xf, axis=-1, keepdims=True)
    var = jnp.mean(jnp.square(xf - mu), axis=-1, keepdims=True)
    y = (xf - mu) * lax.rsqrt(var + LN_EPS)
    return (y * g.astype(jnp.float32) + b.astype(jnp.float32)).astype(x.dtype)


def causal_dwconv(ext, w):
    t = ext.shape[1] - (CONV_W - 1)
    out = ext[:, CONV_W - 1:] * w[CONV_W - 1]
    for j in range(CONV_W - 1):
        out = out + ext[:, j:j + t] * w[j]
    return out


def partial_rope(x, pos):
    d = x.shape[-1]
    rot = d // 4
    half = rot // 2
    inv = jnp.power(jnp.float32(ROPE_THETA), -jnp.arange(half, dtype=jnp.float32) * 2.0 / rot)
    ang = pos.astype(jnp.float32)[:, None] * inv[None, :]
    cos = jnp.cos(ang)[None, :, None, :]
    sin = jnp.sin(ang)[None, :, None, :]
    x1 = x[..., :half].astype(jnp.float32)
    x2 = x[..., half:rot].astype(jnp.float32)
    r1 = (x1 * cos - x2 * sin).astype(x.dtype)
    r2 = (x2 * cos + x1 * sin).astype(x.dtype)
    return jnp.concatenate([r1, r2, x[..., rot:]], axis=-1)


def chunk_spatial_gate(u, v, w_s, b_s):
    n, t, _ = v.shape
    n_ch = -(-t // CHUNK)
    pad = n_ch * CHUNK - t
    vp = jnp.pad(v, ((0, 0), (0, pad), (0, 0))).reshape(n, n_ch, CHUNK, A_GROUPS, A_GROUP_W)
    mask = jnp.tril(jnp.ones((CHUNK, CHUNK), dtype=bool))
    w_c = jnp.where(mask[None], w_s, jnp.zeros_like(w_s))
    mix = jnp.einsum('gts,bnsgc->bntgc', w_c, vp) + b_s.T[:, :, None]
    mix = mix.reshape(n, n_ch * CHUNK, A_W)[:, :t]
    return u * mix


def multiscale_pool(ext, pos):
    t = pos.shape[0]
    csum = jnp.cumsum(ext.astype(jnp.float32), axis=1)
    cs = jnp.concatenate([jnp.zeros_like(csum[:, :1]), csum], axis=1)
    end = cs[:, POOL_HIST + 1:]
    outs = []
    for g, w in enumerate(POOL_WINDOWS):
        sl = slice(g * C_GROUP_W, (g + 1) * C_GROUP_W)
        start = cs[:, POOL_HIST + 1 - w:POOL_HIST + 1 - w + t, sl]
        cnt = jnp.minimum(w, pos + 1).astype(jnp.float32)[None, :, None]
        outs.append((end[..., sl] - start) / cnt)
    mean = jnp.concatenate(outs, axis=-1)
    return (mean - ext[:, POOL_HIST:].astype(jnp.float32)).astype(ext.dtype)


def index_select(qi, wi, ki_all, pos_q, k_sel):
    dots = jax.nn.relu(jnp.einsum('nthd,nsd->nths', qi, ki_all))
    score = jnp.einsum('nth,nths->nts', wi, dots).astype(jnp.float32)
    admissible = jnp.arange(ki_all.shape[1])[None, :] <= pos_q[:, None]
    score = jnp.where(admissible[None], score, -jnp.inf)
    _, idx = lax.top_k(score, k_sel)
    return idx


def sparse_attend(q, ks, vs, idx, pos_q):
    logits = jnp.einsum('nthd,ntkhd->nthk', q, ks).astype(jnp.float32) * ATT_SCALE
    valid = (idx <= pos_q[None, :, None])[:, :, None, :]
    p = jax.nn.softmax(jnp.where(valid, logits, -jnp.inf), axis=-1).astype(q.dtype)
    return jnp.einsum('nthk,ntkhd->nthd', p, vs)


def dsa_prompt(q, k, v, qi, ki, wi):
    n, s = q.shape[:2]
    k_sel = min(TOPK_MAX, s // 4)
    nb = s // Q_BLOCK
    nidx = jnp.arange(n)[:, None, None]

    def block(args):
        qb, qib, wib, start = args
        pos_q = start + jnp.arange(Q_BLOCK, dtype=jnp.int32)
        idx = index_select(qib, wib, ki, pos_q, k_sel)
        return sparse_attend(qb, k[nidx, idx], v[nidx, idx], idx, pos_q)

    to_blocks = lambda a: a.reshape(n, nb, Q_BLOCK, *a.shape[2:]).swapaxes(0, 1)
    starts = jnp.arange(nb, dtype=jnp.int32) * Q_BLOCK
    out = lax.map(block, (to_blocks(q), to_blocks(qi), to_blocks(wi), starts))
    return out.swapaxes(0, 1).reshape(n, s, N_HEADS, HEAD_DIM)


def dsa_sample(q, k, v, qi, ki, wi, *, layer, cache_k, cache_v, cache_kidx, page_table):
    n, t = q.shape[:2]
    past_len = page_table.shape[1] * PAGE_SIZE
    k_sel = min(TOPK_MAX, (past_len + t) // 4)
    ki_past = cache_kidx[layer, page_table].reshape(n, past_len, IDX_DIM)
    ki_all = jnp.concatenate([ki_past, ki], axis=1)
    pos_q = past_len + jnp.arange(t, dtype=jnp.int32)
    idx = index_select(qi, wi, ki_all, pos_q, k_sel)
    nidx = jnp.arange(n)[:, None, None]
    in_past = (idx < past_len)[..., None, None]
    pidx = jnp.minimum(idx, past_len - 1)
    phys = page_table[nidx, pidx // PAGE_SIZE]
    off = pidx % PAGE_SIZE
    nnew = jnp.clip(idx - past_len, 0, t - 1)
    ks = jnp.where(in_past, cache_k[layer, phys, off], k[nidx, nnew])
    vs = jnp.where(in_past, cache_v[layer, phys, off], v[nidx, nnew])
    return sparse_attend(q, ks, vs, idx, pos_q)


def even_mixer(x, conv_hist, w_in, a_ln_g, a_ln_b, a_ws, a_bs, b_conv_w, w_out):
    h = x @ w_in
    u = jax.nn.gelu(h[..., E_U:E_V])
    v = layer_norm(jax.nn.gelu(h[..., E_V:E_B]), a_ln_g, a_ln_b)
    a_out = chunk_spatial_gate(u, v, a_ws, a_bs)
    ext = jnp.concatenate([conv_hist, h[..., E_C:E_H] * h[..., E_H:EVEN_COLS]], axis=1)
    b_out = h[..., E_B:E_C] * causal_dwconv(ext, b_conv_w)
    y = jnp.concatenate([a_out, b_out], axis=-1) @ w_out
    return y, v, ext[:, -(CONV_W - 1):]


def odd_mixer(x, pool_hist, pos, attn_fn, w_in, c_w, c_scale, w_out):
    n, t, _ = x.shape
    h = x @ w_in
    q = partial_rope(h[..., O_Q:O_K].reshape(n, t, N_HEADS, HEAD_DIM), pos)
    k = partial_rope(h[..., O_K:O_V].reshape(n, t, N_HEADS, HEAD_DIM), pos)
    v = h[..., O_V:O_QI].reshape(n, t, N_HEADS, HEAD_DIM)
    qi = partial_rope(h[..., O_QI:O_KI].reshape(n, t, IDX_HEADS, IDX_DIM), pos)
    ki = partial_rope(h[..., O_KI:O_WI][:, :, None, :], pos)[:, :, 0]
    wi = h[..., O_WI:ODD_COLS]
    ext = jnp.concatenate([pool_hist, h[..., O_P:O_Q]], axis=1)
    pooled = multiscale_pool(ext, pos).reshape(n, t, C_GROUPS, C_GROUP_W)
    c_out = jnp.einsum('ntgc,gcd->ntgd', pooled, c_w).reshape(n, t, C_W) * c_scale
    att = attn_fn(q, k, v, qi, ki, wi).reshape(n, t, ATT_W)
    y = jnp.concatenate([c_out, att], axis=-1) @ w_out
    return y, ext[:, -POOL_HIST:], k, v, ki


def conv_ffn(x, hist, w_up, conv_w, conv_b, w_down):
    hu = x @ w_up
    ext = jnp.concatenate([hist, hu], axis=1)
    hc = causal_dwconv(ext, conv_w) + conv_b
    y = (jax.nn.silu(hc[..., :D_FF]) * hc[..., D_FF:]) @ w_down
    return y, ext[:, -(CONV_W - 1):]


def setup_inputs(seed: int = 0) -> dict:
    key = jax.random.key(seed)
    ks = jax.random.split(key, 32)
    n_pages = PAST_LEN // PAGE_SIZE
    n_pool = (5 * DEC_BATCH * n_pages) // 4
    nrm = lambda k, shape, s=1.0: s * jax.random.normal(k, shape, jnp.float32)
    perm = jax.random.permutation(ks[0], n_pool)
    page_table = perm[:DEC_BATCH * n_pages].reshape(DEC_BATCH, n_pages).astype(jnp.int32)
    return {
        'x_prompt': nrm(ks[1], (BATCH, SEQ, D_MODEL)),
        'x_sample': nrm(ks[2], (DEC_BATCH, DEC_SEQ, D_MODEL)),
        'state_b_conv': nrm(ks[3], (N_EVEN, DEC_BATCH, CONV_W - 1, B_W)),
        'state_c_pool': nrm(ks[4], (N_ODD, DEC_BATCH, POOL_HIST, C_W)),
        'cache_k': nrm(ks[5], (N_ODD, n_pool, PAGE_SIZE, N_HEADS, HEAD_DIM)),
        'cache_v': nrm(ks[6], (N_ODD, n_pool, PAGE_SIZE, N_HEADS, HEAD_DIM)),
        'cache_kidx': nrm(ks[7], (N_ODD, n_pool, PAGE_SIZE, IDX_DIM)),
        'state_ffn_conv': nrm(ks[8], (DEPTH, DEC_BATCH, CONV_W - 1, 2 * D_FF)),
        'page_table': page_table,
        'w_in_even': nrm(ks[9], (N_EVEN, D_MODEL, EVEN_COLS), D_MODEL ** -0.5),
        'a_ln_g': 1.0 + nrm(ks[10], (N_EVEN, A_W), 0.1),
        'a_ln_b': nrm(ks[11], (N_EVEN, A_W), 0.1),
        'a_ws': nrm(ks[12], (N_EVEN, A_GROUPS, CHUNK, CHUNK), CHUNK ** -0.5),
        'a_bs': 1.0 + nrm(ks[13], (N_EVEN, A_GROUPS, CHUNK), 0.1),
        'b_conv_w': nrm(ks[14], (N_EVEN, CONV_W, B_W), CONV_W ** -0.5),
        'w_out_even': nrm(ks[15], (N_EVEN, A_W + B_W, D_MODEL), BETA * (A_W + B_W) ** -0.5),
        'w_in_odd': nrm(ks[16], (N_ODD, D_MODEL, ODD_COLS), D_MODEL ** -0.5),
        'c_w': nrm(ks[17], (N_ODD, C_GROUPS, C_GROUP_W, C_GROUP_W), C_GROUP_W ** -0.5),
        'c_scale': 1.0 + nrm(ks[18], (N_ODD, C_W), 0.1),
        'w_out_odd': nrm(ks[19], (N_ODD, C_W + ATT_W, D_MODEL), BETA * (C_W + ATT_W) ** -0.5),
        'ln_mix_g': 1.0 + nrm(ks[20], (DEPTH, D_MODEL), 0.1),
        'ln_mix_b': nrm(ks[21], (DEPTH, D_MODEL), 0.1),
        'ffn_w_up': nrm(ks[22], (DEPTH, D_MODEL, 2 * D_FF), D_MODEL ** -0.5),
        'ffn_conv_w': nrm(ks[23], (DEPTH, CONV_W, 2 * D_FF), CONV_W ** -0.5),
        'ffn_conv_b': nrm(ks[24], (DEPTH, 2 * D_FF), 0.02),
        'ffn_w_down': nrm(ks[25], (DEPTH, D_FF, D_MODEL), BETA * D_FF ** -0.5),
        'ln_ffn_g': 1.0 + nrm(ks[26], (DEPTH, D_MODEL), 0.1),
        'ln_ffn_b': nrm(ks[27], (DEPTH, D_MODEL), 0.1),
    }


def reference(x_prompt, x_sample, state_b_conv, state_c_pool, cache_k, cache_v, cache_kidx,
              state_ffn_conv, page_table, w_in_even, a_ln_g, a_ln_b, a_ws, a_bs, b_conv_w,
              w_out_even, w_in_odd, c_w, c_scale, w_out_odd, ln_mix_g, ln_mix_b, ffn_w_up,
              ffn_conv_w, ffn_conv_b, ffn_w_down, ln_ffn_g, ln_ffn_b):
    xp, xs = x_prompt, x_sample
    n_p, seq = xp.shape[:2]
    n_s, t_s = xs.shape[:2]
    pos_p = jnp.arange(seq, dtype=jnp.int32)
    pos_s = PAST_LEN + jnp.arange(t_s, dtype=jnp.int32)
    a_v_s, b_p, b_s, c_p, c_s = [], [], [], [], []
    k_p, v_p, ki_p, k_s, v_s, ki_s = [], [], [], [], [], []
    f_p, f_s = [], []
    for layer in range(DEPTH):
        if layer % 2 == 0:
            e = layer // 2
            wts = (w_in_even[e], a_ln_g[e], a_ln_b[e], a_ws[e], a_bs[e], b_conv_w[e], w_out_even[e])
            yp, _, bp = even_mixer(xp, jnp.zeros((n_p, CONV_W - 1, B_W), xp.dtype), *wts)
            ys, av, bs = even_mixer(xs, state_b_conv[e], *wts)
            a_v_s.append(av)
            b_p.append(bp)
            b_s.append(bs)
        else:
            o = layer // 2
            wts = (w_in_odd[o], c_w[o], c_scale[o], w_out_odd[o])
            yp, cp, kp, vp, kip = odd_mixer(xp, jnp.zeros((n_p, POOL_HIST, C_W), xp.dtype),
                                            pos_p, dsa_prompt, *wts)
            attn_s = functools.partial(dsa_sample, layer=o, cache_k=cache_k, cache_v=cache_v,
                                       cache_kidx=cache_kidx, page_table=page_table)
            ys, cs, kss, vss, kis = odd_mixer(xs, state_c_pool[o], pos_s, attn_s, *wts)
            c_p.append(cp)
            c_s.append(cs)
            k_p.append(kp)
            v_p.append(vp)
            ki_p.append(kip)
            k_s.append(kss)
            v_s.append(vss)
            ki_s.append(kis)
        xp = layer_norm(ALPHA * xp + yp, ln_mix_g[layer], ln_mix_b[layer])
        xs = layer_norm(ALPHA * xs + ys, ln_mix_g[layer], ln_mix_b[layer])
        fw = (ffn_w_up[layer], ffn_conv_w[layer], ffn_conv_b[layer], ffn_w_down[layer])
        fp, hp = conv_ffn(xp, jnp.zeros((n_p, CONV_W - 1, 2 * D_FF), xp.dtype), *fw)
        fs, hs = conv_ffn(xs, state_ffn_conv[layer], *fw)
        f_p.append(hp)
        f_s.append(hs)
        xp = layer_norm(ALPHA * xp + fp, ln_ffn_g[layer], ln_ffn_b[layer])
        xs = layer_norm(ALPHA * xs + fs, ln_ffn_g[layer], ln_ffn_b[layer])
    return (xp, xs, jnp.stack(a_v_s), jnp.stack(b_p), jnp.stack(b_s), jnp.stack(c_p), jnp.stack(c_s),
            jnp.stack(k_p), jnp.stack(v_p), jnp.stack(ki_p), jnp.stack(k_s), jnp.stack(v_s), jnp.stack(ki_s),
            jnp.stack(f_p), jnp.stack(f_s))
```

```python
import functools

import jax
import jax.numpy as jnp
from jax import lax
from jax.experimental import pallas as pl
from jax.experimental.pallas import tpu as pltpu

D_MODEL = 1024
BATCH = 4
SEQ = 4096
DEPTH = 2
DEC_BATCH = 128
DEC_SEQ = 4
PAST_LEN = 2048
PAGE_SIZE = 128
N_PAGES = PAST_LEN // PAGE_SIZE

A_W = D_MODEL // 2
B_W = D_MODEL // 2
C_W = D_MODEL // 2
ATT_W = D_MODEL // 2
CHUNK = 128
A_GROUPS = A_W // CHUNK
CONV_W = 3
POOL_WINDOWS = (2, 4, 8, 16)
C_GROUPS = len(POOL_WINDOWS)
C_GROUP_W = C_W // C_GROUPS
POOL_HIST = max(POOL_WINDOWS) - 1
HEAD_DIM = 64
N_HEADS = ATT_W // HEAD_DIM
IDX_HEADS = 8
IDX_DIM = 64
TOPK = 256
Q_BLOCK = 128
ROPE_THETA = 500000.0
ROT = HEAD_DIM // 4
D_FF = ((8 * D_MODEL // 3 + 127) // 128) * 128
ALPHA = (2.0 * DEPTH) ** 0.25
LN_EPS = 1e-5
ATT_SCALE = HEAD_DIM ** -0.5

E_U, E_V, E_B, E_C, E_H = 0, A_W, 2 * A_W, 2 * A_W + B_W, 2 * A_W + 2 * B_W
O_P, O_Q, O_K, O_V, O_QI = 0, C_W, C_W + ATT_W, C_W + 2 * ATT_W, C_W + 3 * ATT_W
O_KI = O_QI + IDX_HEADS * IDX_DIM
O_WI = O_KI + IDX_DIM

LANES = 128
SUBLANES = 8
VMEM_LIMIT = 56 * 1024 * 1024

TM = 512
FF_CHUNK = 256
KEY_CHUNK = 512
S_ROWS = DEC_SEQ * DEC_BATCH
INT_MIN = -2 ** 31
NEG_BIG = -1e30

bf16 = jnp.bfloat16
f32 = jnp.float32


def _ln(x, g, b):
    mu = jnp.mean(x, axis=-1, keepdims=True)
    xc = x - mu
    var = jnp.mean(xc * xc, axis=-1, keepdims=True)
    return xc * lax.rsqrt(var + LN_EPS) * g + b


def _gelu(x):
    c = (2.0 / jnp.pi) ** 0.5
    return x * (0.5 * (1.0 + jnp.tanh(c * (x + 0.044715 * (x * x * x)))))


def _dot(a, b):
    return jnp.dot(a, b, preferred_element_type=f32)


def _dot_nt(a, b):
    return lax.dot_general(a, b, (((1,), (1,)), ((), ())), preferred_element_type=f32)


def _prev_rows_prompt(cur, halo):
    ext = jnp.concatenate([halo, cur], axis=0)
    p1 = pltpu.roll(ext, 1, 0)[SUBLANES:]
    p2 = pltpu.roll(ext, 2, 0)[SUBLANES:]
    return p1, p2


def _prev_rows_sample(cur, hist):
    n = DEC_BATCH
    p1 = jnp.concatenate([hist[n:2 * n], cur[0:3 * n]], axis=0)
    p2 = jnp.concatenate([hist, cur[0:2 * n]], axis=0)
    return p1, p2


def _causal_conv(cur, p1, p2, w_ref, c0, width):
    return (cur * w_ref[2:3, c0:c0 + width] + p1 * w_ref[1:2, c0:c0 + width]
            + p2 * w_ref[0:1, c0:c0 + width])


def _even_pre_kernel(*refs, sample):
    if sample:
        (x_ref, w_ref, lg_ref, lb_ref, wcoef_ref, bias_ref, cw_ref, hist_ref,
         a_ref, b_ref, v_ref, e_ref) = refs
    else:
        (x_ref, w_ref, lg_ref, lb_ref, ws_ref, bias_ref, cw_ref,
         a_ref, b_ref, st_ref, halo_ref) = refs

        @pl.when(pl.program_id(1) == 0)
        def _():
            halo_ref[...] = jnp.zeros_like(halo_ref)

    xb = x_ref[...].astype(bf16)
    rows = xb.shape[0]

    def proj(c0):
        return _dot(xb, w_ref[:, c0:c0 + A_W])

    u = _gelu(proj(E_U))
    v = _ln(_gelu(proj(E_V)), lg_ref[...], lb_ref[...])
    if sample:
        v_ref[...] = v
        n = DEC_BATCH
        for t in range(DEC_SEQ):
            mix = bias_ref[t:t + 1, :]
            for s in range(t + 1):
                r = t * DEC_SEQ + s
                mix = mix + wcoef_ref[r:r + 1, :] * v[s * n:(s + 1) * n]
            a_ref[t * n:(t + 1) * n, :] = (u[t * n:(t + 1) * n] * mix).astype(bf16)
    else:
        vb = v.astype(bf16)
        ri = lax.broadcasted_iota(jnp.int32, (CHUNK, CHUNK), 0)
        ci = lax.broadcasted_iota(jnp.int32, (CHUNK, CHUNK), 1)
        for g in range(A_GROUPS):
            gs = slice(g * CHUNK, (g + 1) * CHUNK)
            wc = jnp.where(ri >= ci, ws_ref[g], 0.0).astype(bf16)
            for c in range(rows // CHUNK):
                rs = slice(c * CHUNK, (c + 1) * CHUNK)
                mix = _dot(wc, vb[rs, gs]) + bias_ref[:, gs]
                a_ref[rs, gs] = (u[rs, gs] * mix).astype(bf16)

    e = proj(E_C) * proj(E_H)
    if sample:
        e_ref[...] = e
        p1, p2 = _prev_rows_sample(e, hist_ref[...])
    else:
        p1, p2 = _prev_rows_prompt(e, halo_ref[...])
        halo_ref[...] = e[rows - SUBLANES:]
        st_ref[...] = e[rows - SUBLANES:]
    conv = _causal_conv(e, p1, p2, cw_ref, 0, B_W)
    b_ref[...] = (proj(E_B) * conv).astype(bf16)


def _full(shape):
    nd = len(shape)
    return pl.BlockSpec(shape, lambda *_: (0,) * nd)


def _even_pre_prompt(x, w, lg, lb, ws, bias, cw):
    nt = SEQ // TM
    row = lambda c: pl.BlockSpec((None, TM, c), lambda b, t: (b, t, 0))
    return pl.pallas_call(
        functools.partial(_even_pre_kernel, sample=False),
        grid=(BATCH, nt),
        in_specs=[row(D_MODEL), _full(w.shape), _full(lg.shape), _full(lb.shape), _full(ws.shape),
                  _full(bias.shape), _full(cw.shape)],
        out_specs=[row(A_W), row(B_W), pl.BlockSpec((None, SUBLANES, B_W), lambda b, t: (b, 0, 0))],
        out_shape=[jax.ShapeDtypeStruct((BATCH, SEQ, A_W), bf16),
                   jax.ShapeDtypeStruct((BATCH, SEQ, B_W), bf16),
                   jax.ShapeDtypeStruct((BATCH, SUBLANES, B_W), f32)],
        scratch_shapes=[pltpu.VMEM((SUBLANES, B_W), f32)],
        compiler_params=pltpu.CompilerParams(dimension_semantics=("arbitrary", "arbitrary"),
                                             vmem_limit_bytes=VMEM_LIMIT),
        name="even_pre_prompt",
    )(x, w, lg, lb, ws, bias, cw)


def _even_pre_sample(x, w, lg, lb, wcoef, bias, cw, hist):
    args = (x, w, lg, lb, wcoef, bias, cw, hist)
    return pl.pallas_call(
        functools.partial(_even_pre_kernel, sample=True),
        grid=(1,),
        in_specs=[_full(a.shape) for a in args],
        out_specs=[_full((S_ROWS, A_W)), _full((S_ROWS, B_W)), _full((S_ROWS, A_W)), _full((S_ROWS, B_W))],
        out_shape=[jax.ShapeDtypeStruct((S_ROWS, A_W), bf16),
                   jax.ShapeDtypeStruct((S_ROWS, B_W), bf16),
                   jax.ShapeDtypeStruct((S_ROWS, A_W), f32),
                   jax.ShapeDtypeStruct((S_ROWS, B_W), f32)],
        compiler_params=pltpu.CompilerParams(dimension_semantics=("arbitrary",),
                                             vmem_limit_bytes=VMEM_LIMIT),
        name="even_pre_sample",
    )(*args)


def _post_kernel(*refs, sample):
    if sample:
        (x_ref, ca_ref, cb_ref, wo_ref, g1_ref, b1_ref, wu_ref, cw_ref, cb2_ref, wd_ref, g2_ref, b2_ref,
         hist_ref, o_ref, st_ref, acc_ref) = refs
    else:
        (x_ref, ca_ref, cb_ref, wo_ref, g1_ref, b1_ref, wu_ref, cw_ref, cb2_ref, wd_ref, g2_ref, b2_ref,
         o_ref, st_ref, acc_ref, halo_ref) = refs

        @pl.when(pl.program_id(1) == 0)
        def _():
            halo_ref[...] = jnp.zeros_like(halo_ref)

    half = ca_ref.shape[-1]
    y = _dot(ca_ref[...], wo_ref[0:half, :]) + _dot(cb_ref[...], wo_ref[half:2 * half, :])
    x1 = _ln(ALPHA * x_ref[...] + y, g1_ref[...], b1_ref[...])
    x1b = x1.astype(bf16)
    rows = x1b.shape[0]

    def conv_half(c0):
        hu = _dot(x1b, wu_ref[:, c0:c0 + FF_CHUNK])
        if sample:
            p1, p2 = _prev_rows_sample(hu, hist_ref[:, c0:c0 + FF_CHUNK])
            st_ref[:, c0:c0 + FF_CHUNK] = hu[2 * DEC_BATCH:]
        else:
            p1, p2 = _prev_rows_prompt(hu, halo_ref[:, c0:c0 + FF_CHUNK])
            halo_ref[:, c0:c0 + FF_CHUNK] = hu[rows - SUBLANES:]
            st_ref[:, c0:c0 + FF_CHUNK] = hu[rows - SUBLANES:]
        return _causal_conv(hu, p1, p2, cw_ref, c0, FF_CHUNK) + cb2_ref[:, c0:c0 + FF_CHUNK]

    for c in range(D_FF // FF_CHUNK):
        c0 = c * FF_CHUNK
        hg = conv_half(c0)
        hv = conv_half(D_FF + c0)
        act = (hg * jax.nn.sigmoid(hg) * hv).astype(bf16)
        part = _dot(act, wd_ref[c0:c0 + FF_CHUNK, :])
        if c == 0:
            acc_ref[...] = part
        else:
            acc_ref[...] += part
    o_ref[...] = _ln(ALPHA * x1 + acc_ref[...], g2_ref[...], b2_ref[...])


def _post_prompt(x, ca, cb, wo, g1, b1, wu, cw, cb2, wd, g2, b2):
    nt = SEQ // TM
    row = lambda c: pl.BlockSpec((None, TM, c), lambda b, t: (b, t, 0))
    const = lambda a: pl.BlockSpec(a.shape, lambda b, t: (0,) * a.ndim, pipeline_mode=pl.Buffered(1))
    return pl.pallas_call(
        functools.partial(_post_kernel, sample=False),
        grid=(BATCH, nt),
        in_specs=[row(D_MODEL), row(ca.shape[-1]), row(cb.shape[-1])]
                 + [const(a) for a in (wo, g1, b1, wu, cw, cb2, wd, g2, b2)],
        out_specs=[row(D_MODEL), pl.BlockSpec((None, SUBLANES, 2 * D_FF), lambda b, t: (b, 0, 0))],
        out_shape=[jax.ShapeDtypeStruct((BATCH, SEQ, D_MODEL), f32),
                   jax.ShapeDtypeStruct((BATCH, SUBLANES, 2 * D_FF), f32)],
        scratch_shapes=[pltpu.VMEM((TM, D_MODEL), f32), pltpu.VMEM((SUBLANES, 2 * D_FF), f32)],
        compiler_params=pltpu.CompilerParams(dimension_semantics=("arbitrary", "arbitrary"),
                                             vmem_limit_bytes=VMEM_LIMIT),
        name="post_prompt",
    )(x, ca, cb, wo, g1, b1, wu, cw, cb2, wd, g2, b2)


def _post_sample(x, ca, cb, wo, g1, b1, wu, cw, cb2, wd, g2, b2, hist):
    args = (x, ca, cb, wo, g1, b1, wu, cw, cb2, wd, g2, b2, hist)
    return pl.pallas_call(
        functools.partial(_post_kernel, sample=True),
        grid=(1,),
        in_specs=[_full(a.shape) for a in args],
        out_specs=[_full((S_ROWS, D_MODEL)), _full((2 * DEC_BATCH, 2 * D_FF))],
        out_shape=[jax.ShapeDtypeStruct((S_ROWS, D_MODEL), f32),
                   jax.ShapeDtypeStruct((2 * DEC_BATCH, 2 * D_FF), f32)],
        scratch_shapes=[pltpu.VMEM((S_ROWS, D_MODEL), f32)],
        compiler_params=pltpu.CompilerParams(dimension_semantics=("arbitrary",),
                                             vmem_limit_bytes=VMEM_LIMIT),
        name="post_sample",
    )(*args)


def _rope(blk, cos, s1, s2):
    return blk * cos + pltpu.roll(blk, ROT // 2, 1) * s1 + pltpu.roll(blk, LANES - ROT // 2, 1) * s2


def _odd_pre_kernel(*refs, sample):
    if sample:
        (x_ref, wm_ref, wt_ref, cos_ref, s1_ref, s2_ref, cwt_ref, cs_ref, hist_ref,
         k_ref, v_ref, ki_ref, q_ref, qi_ref, wi_ref, co_ref, p_ref) = refs
    else:
        (x_ref, wm_ref, wt_ref, cos_ref, s1_ref, s2_ref, cwt_ref, cs_ref,
         k_ref, v_ref, ki_ref, qb_ref, kb_ref, vb_ref, qib_ref, ki2_ref, wi_ref, co_ref, ph_ref,
         halo_ref) = refs

        @pl.when(pl.program_id(1) == 0)
        def _():
            halo_ref[...] = jnp.zeros_like(halo_ref)

    xb = x_ref[...].astype(bf16)
    rows = xb.shape[0]
    cos, s1, s2 = cos_ref[...], s1_ref[...], s2_ref[...]

    def proj(c0):
        return _dot(xb, wm_ref[:, c0:c0 + C_W])

    def rope_wide(h):
        return jnp.concatenate(
            [_rope(h[:, j * LANES:(j + 1) * LANES], cos, s1, s2) for j in range(h.shape[1] // LANES)], axis=1)

    q = rope_wide(proj(O_Q))
    k = rope_wide(proj(O_K))
    v = proj(O_V)
    qi = rope_wide(proj(O_QI))
    tail = _dot(xb, wt_ref[...])
    ki2 = _rope(tail[:, 0:LANES], cos, s1, s2)
    k_ref[...] = k
    v_ref[...] = v
    ki_ref[...] = ki2[:, 0:IDX_DIM]
    wi_ref[...] = tail[:, LANES:2 * LANES]
    if sample:
        q_ref[...] = q
        qi_ref[...] = qi
    else:
        qb_ref[...] = (q * ATT_SCALE).astype(bf16)
        kb_ref[...] = k.astype(bf16)
        vb_ref[...] = v.astype(bf16)
        qib_ref[...] = qi.astype(bf16)
        ki2_ref[...] = ki2.astype(bf16)

    p = proj(O_P)
    if sample:
        p_ref[...] = p
        n = DEC_BATCH
        hist = hist_ref[...]

        def slab(j, gs):
            if j < POOL_HIST:
                return hist[j * n:(j + 1) * n, gs]
            return p[(j - POOL_HIST) * n:(j - POOL_HIST + 1) * n, gs]

        for g, w in enumerate(POOL_WINDOWS):
            gs = slice(g * C_GROUP_W, (g + 1) * C_GROUP_W)
            for t in range(DEC_SEQ):
                win = slab(POOL_HIST + t, gs)
                for i in range(1, w):
                    win = win + slab(POOL_HIST + t - i, gs)
                pooled = win / float(w) - p[t * n:(t + 1) * n, gs]
                co = _dot(pooled.astype(bf16), cwt_ref[g]) * cs_ref[:, gs]
                co_ref[t * n:(t + 1) * n, gs] = co.astype(bf16)
    else:
        hrows = 2 * SUBLANES
        ext = jnp.concatenate([halo_ref[...], p], axis=0)
        pos = pl.program_id(1) * rows + lax.broadcasted_iota(jnp.int32, (rows, 1), 0)
        for g, w in enumerate(POOL_WINDOWS):
            gs = slice(g * C_GROUP_W, (g + 1) * C_GROUP_W)
            s = ext[:, gs]
            step = 1
            while step < w:
                s = s + pltpu.roll(s, step, 0)
                step *= 2
            cnt = jnp.minimum(w, pos + 1).astype(f32)
            pooled = s[hrows:] / cnt - p[:, gs]
            co = _dot(pooled.astype(bf16), cwt_ref[g]) * cs_ref[:, gs]
            co_ref[:, gs] = co.astype(bf16)
        halo_ref[...] = p[rows - hrows:]
        ph_ref[...] = p[rows - hrows:]


def _odd_pre_prompt(x, wm, wt, cos, s1, s2, cwt, cs):
    nt = SEQ // TM
    row = lambda c: pl.BlockSpec((None, TM, c), lambda b, t: (b, t, 0))
    tab = pl.BlockSpec((TM, LANES), lambda b, t: (t, 0))
    sds = lambda c, dt: jax.ShapeDtypeStruct((BATCH, SEQ, c), dt)
    hrows = 2 * SUBLANES
    return pl.pallas_call(
        functools.partial(_odd_pre_kernel, sample=False),
        grid=(BATCH, nt),
        in_specs=[row(D_MODEL), _full(wm.shape), _full(wt.shape), tab, tab, tab, _full(cwt.shape),
                  _full(cs.shape)],
        out_specs=[row(ATT_W), row(ATT_W), row(IDX_DIM), row(ATT_W), row(ATT_W), row(ATT_W), row(ATT_W),
                   row(LANES), row(LANES), row(C_W),
                   pl.BlockSpec((None, hrows, C_W), lambda b, t: (b, 0, 0))],
        out_shape=[sds(ATT_W, f32), sds(ATT_W, f32), sds(IDX_DIM, f32), sds(ATT_W, bf16), sds(ATT_W, bf16),
                   sds(ATT_W, bf16), sds(ATT_W, bf16), sds(LANES, bf16), sds(LANES, f32), sds(C_W, bf16),
                   jax.ShapeDtypeStruct((BATCH, hrows, C_W), f32)],
        scratch_shapes=[pltpu.VMEM((hrows, C_W), f32)],
        compiler_params=pltpu.CompilerParams(dimension_semantics=("arbitrary", "arbitrary"),
                                             vmem_limit_bytes=VMEM_LIMIT),
        name="odd_pre_prompt",
    )(x, wm, wt, cos, s1, s2, cwt, cs)


def _odd_pre_sample(x, wm, wt, cos, s1, s2, cwt, cs, hist):
    args = (x, wm, wt, cos, s1, s2, cwt, cs, hist)
    sds = lambda c, dt: jax.ShapeDtypeStruct((S_ROWS, c), dt)
    outs = [sds(ATT_W, f32), sds(ATT_W, f32), sds(IDX_DIM, f32), sds(ATT_W, f32), sds(ATT_W, f32),
            sds(LANES, f32), sds(C_W, bf16), sds(C_W, f32)]
    return pl.pallas_call(
        functools.partial(_odd_pre_kernel, sample=True),
        grid=(1,),
        in_specs=[_full(a.shape) for a in args],
        out_specs=[_full(o.shape) for o in outs],
        out_shape=outs,
        compiler_params=pltpu.CompilerParams(dimension_semantics=("arbitrary",),
                                             vmem_limit_bytes=VMEM_LIMIT),
        name="odd_pre_sample",
    )(*args)


def _order_key(s):
    s = jnp.where(s == 0.0, 0.0, s)
    bits = lax.bitcast_convert_type(s, jnp.int32)
    return bits ^ ((bits >> 31) & 0x7FFFFFFF)


def _select_topk(count_ge, count_tie_lt, shape, idx_bits):
    k = float(TOPK)
    t0 = jnp.full(shape, INT_MIN, jnp.int32)
    zero = jnp.zeros(shape, jnp.int32)
    t0 = jnp.where(count_ge(zero) >= k, zero, t0)

    def vbit(i, t):
        cand = t | lax.shift_left(jnp.int32(1), 30 - i)
        return jnp.where(count_ge(cand) >= k, cand, t)

    thr = lax.fori_loop(0, 31, vbit, t0)
    need = k - count_ge(thr + 1)

    def jbit(i, j):
        cand = j | lax.shift_left(jnp.int32(1), idx_bits - 1 - i)
        return jnp.where(count_tie_lt(thr, cand) <= need, cand, j)

    jsel = lax.fori_loop(0, idx_bits, jbit, zero)
    return thr, jsel


def _dsa_prompt_kernel(q_ref, qi_ref, wi_ref, k_ref, v_ref, ki2_ref, o_ref,
                       keys_ref, m_ref, l_ref, acc_ref):
    qb = pl.program_id(1)
    n_chunks = qb // (KEY_CHUNK // Q_BLOCK) + 1
    lane = lax.broadcasted_iota(jnp.int32, (1, LANES), 1)
    low = lane < HEAD_DIM
    qpos = qb * Q_BLOCK + lax.broadcasted_iota(jnp.int32, (Q_BLOCK, 1), 0)
    kiota = lax.broadcasted_iota(jnp.int32, (1, KEY_CHUNK), 1)

    def head_masked(x, h):
        blk = x[:, (h // 2) * LANES:(h // 2 + 1) * LANES]
        return jnp.where(low if h % 2 == 0 else jnp.logical_not(low), blk, jnp.zeros_like(blk))

    qi = qi_ref[...]
    qstack = jnp.concatenate([head_masked(qi, h) for h in range(IDX_HEADS)], axis=0)
    wi = wi_ref[...]

    def score_body(c, carry):
        ks = pl.multiple_of(c * KEY_CHUNK, KEY_CHUNK)
        d = _dot_nt(qstack, ki2_ref[pl.ds(ks, KEY_CHUNK), :])
        s = jnp.zeros((Q_BLOCK, KEY_CHUNK), f32)
        for h in range(IDX_HEADS):
            s = s + wi[:, h:h + 1] * jnp.maximum(d[h * Q_BLOCK:(h + 1) * Q_BLOCK], 0.0)
        adm = (ks + kiota) <= qpos
        keys_ref[:, pl.ds(ks, KEY_CHUNK)] = jnp.where(adm, _order_key(s), INT_MIN)
        return carry

    lax.fori_loop(0, n_chunks, score_body, 0)

    def lane_fold(m):
        acc = m[:, 0:LANES]
        for j in range(1, KEY_CHUNK // LANES):
            acc = acc + m[:, j * LANES:(j + 1) * LANES]
        return acc

    def count(pred):
        def body(c, acc):
            ks = pl.multiple_of(c * KEY_CHUNK, KEY_CHUNK)
            kk = keys_ref[:, pl.ds(ks, KEY_CHUNK)]
            return acc + lane_fold(jnp.where(pred(kk, ks + kiota), 1.0, 0.0))
        acc = lax.fori_loop(0, n_chunks, body, jnp.zeros((Q_BLOCK, LANES), f32))
        return jnp.sum(acc, axis=1, keepdims=True)

    thr, jsel = _select_topk(
        lambda cand: count(lambda kk, kp: kk >= cand),
        lambda t, j: count(lambda kk, kp: jnp.logical_and(kk == t, kp < j)),
        (Q_BLOCK, 1), 13)

    m_ref[...] = jnp.full(m_ref.shape, NEG_BIG, f32)
    l_ref[...] = jnp.zeros_like(l_ref)
    acc_ref[...] = jnp.zeros_like(acc_ref)
    q = q_ref[...]
    qm = [head_masked(q, h) for h in range(N_HEADS)]

    def att_body(c, carry):
        ks = pl.multiple_of(c * KEY_CHUNK, KEY_CHUNK)
        kk = keys_ref[:, pl.ds(ks, KEY_CHUNK)]
        kp = ks + kiota
        sel = jnp.logical_and(
            jnp.logical_or(kk > thr, jnp.logical_and(kk == thr, kp < jsel)), kk != INT_MIN)
        for h in range(N_HEADS):
            hs = slice((h // 2) * LANES, (h // 2 + 1) * LANES)
            lg = _dot_nt(qm[h], k_ref[pl.ds(ks, KEY_CHUNK), hs])
            m_old = m_ref[h]
            m_new = jnp.maximum(m_old, jnp.max(jnp.where(sel, lg, NEG_BIG), axis=1, keepdims=True))
            p = jnp.where(sel, jnp.exp(lg - m_new), 0.0)
            a = jnp.exp(m_old - m_new)
            l_ref[h] = a * l_ref[h] + jnp.sum(p, axis=1, keepdims=True)
            m_ref[h] = m_new
            acc_ref[h] = a * acc_ref[h] + _dot(p.astype(bf16), v_ref[pl.ds(ks, KEY_CHUNK), hs])
        return carry

    lax.fori_loop(0, n_chunks, att_body, 0)
    for j in range(N_HEADS // 2):
        even = acc_ref[2 * j] / l_ref[2 * j]
        odd = acc_ref[2 * j + 1] / l_ref[2 * j + 1]
        o_ref[:, j * LANES:(j + 1) * LANES] = jnp.where(low, even, odd).astype(bf16)


def _dsa_prompt(qb, qib, wi, kb, vb, ki2b):
    nq = SEQ // Q_BLOCK
    qrow = lambda c: pl.BlockSpec((None, Q_BLOCK, c), lambda b, t: (b, t, 0))
    seq = lambda c: pl.BlockSpec((None, SEQ, c), lambda b, t: (b, 0, 0))
    return pl.pallas_call(
        _dsa_prompt_kernel,
        grid=(BATCH, nq),
        in_specs=[qrow(ATT_W), qrow(ATT_W), qrow(LANES), seq(ATT_W), seq(ATT_W), seq(LANES)],
        out_specs=qrow(ATT_W),
        out_shape=jax.ShapeDtypeStruct((BATCH, SEQ, ATT_W), bf16),
        scratch_shapes=[pltpu.VMEM((Q_BLOCK, SEQ), jnp.int32),
                        pltpu.VMEM((N_HEADS, Q_BLOCK, 1), f32),
                        pltpu.VMEM((N_HEADS, Q_BLOCK, 1), f32),
                        pltpu.VMEM((N_HEADS, Q_BLOCK, LANES), f32)],
        compiler_params=pltpu.CompilerParams(dimension_semantics=("arbitrary", "arbitrary"),
                                             vmem_limit_bytes=VMEM_LIMIT),
        name="dsa_prompt",
    )(qb, qib, wi, kb, vb, ki2b)


S_KEYS = PAST_LEN + PAGE_SIZE


def _dsa_sample_kernel(pt_ref, q_ref, qi_ref, wi_ref, kin_ref, kn_ref, vn_ref, *rest):
    kidx_refs = rest[0:N_PAGES]
    k_refs = rest[N_PAGES:2 * N_PAGES]
    v_refs = rest[2 * N_PAGES:3 * N_PAGES]
    o_ref = rest[3 * N_PAGES]
    del pt_ref
    rows = DEC_SEQ * IDX_HEADS

    def pad_keys(x):
        return jnp.concatenate([x, jnp.zeros((PAGE_SIZE - x.shape[0], x.shape[1]), x.dtype)], axis=0)

    qi = qi_ref[...].astype(bf16)
    wi = wi_ref[...]

    def idx_score(kd):
        r = jnp.maximum(_dot_nt(qi, kd.astype(bf16)), 0.0) * wi
        return jnp.sum(r.reshape(DEC_SEQ, IDX_HEADS, PAGE_SIZE), axis=1)

    parts = [idx_score(kidx_refs[j][...]) for j in range(N_PAGES)]
    parts.append(idx_score(pad_keys(kin_ref[...])))
    s = jnp.concatenate(parts, axis=1)
    kpos = lax.broadcasted_iota(jnp.int32, (1, S_KEYS), 1)
    tq = lax.broadcasted_iota(jnp.int32, (DEC_SEQ, 1), 0)
    adm = (kpos - PAST_LEN) <= tq
    keys = jnp.where(adm, _order_key(s), INT_MIN)

    def count(pred):
        return jnp.sum(jnp.where(pred, 1.0, 0.0), axis=1, keepdims=True)

    thr, jsel = _select_topk(
        lambda cand: count(keys >= cand),
        lambda t, j: count(jnp.logical_and(keys == t, kpos < j)),
        (DEC_SEQ, 1), 13)
    sel = jnp.logical_and(
        jnp.logical_or(keys > thr, jnp.logical_and(keys == thr, kpos < jsel)), keys != INT_MIN)
    self32 = jnp.broadcast_to(jnp.where(sel, 1.0, 0.0)[:, None, :], (DEC_SEQ, N_HEADS, S_KEYS))
    selx = self32.reshape(rows, S_KEYS) > 0.5

    q4 = q_ref[...] * ATT_SCALE
    q32 = jnp.broadcast_to(q4[:, None, :], (DEC_SEQ, N_HEADS, ATT_W)).reshape(rows, ATT_W)
    hrow = lax.broadcasted_iota(jnp.int32, (rows, ATT_W), 0) % N_HEADS
    hcol = lax.broadcasted_iota(jnp.int32, (rows, ATT_W), 1) // HEAD_DIM
    own = hrow == hcol
    qbd = jnp.where(own, q32, 0.0).astype(bf16)

    kn = pad_keys(kn_ref[...]).astype(bf16)
    vn = pad_keys(vn_ref[...]).astype(bf16)
    lg = jnp.concatenate([_dot_nt(qbd, k_refs[j][...].astype(bf16)) for j in range(N_PAGES)]
                         + [_dot_nt(qbd, kn)], axis=1)
    m = jnp.max(jnp.where(selx, lg, NEG_BIG), axis=1, keepdims=True)
    p = jnp.where(selx, jnp.exp(lg - m), 0.0)
    l = jnp.sum(p, axis=1, keepdims=True)
    pb = p.astype(bf16)
    pv = _dot(pb[:, PAST_LEN:], vn)
    for j in range(N_PAGES):
        pv = pv + _dot(pb[:, j * PAGE_SIZE:(j + 1) * PAGE_SIZE], v_refs[j][...].astype(bf16))
    out = jnp.where(own, pv, 0.0) / l
    o_ref[...] = jnp.sum(out.reshape(DEC_SEQ, N_HEADS, ATT_W), axis=1)


def _dsa_sample(page_table, q, qi, wi, ki_new, k_new, v_new, kidx_pages, k_pages, v_pages, layer):
    n_pool = kidx_pages.shape[0] // (DEPTH // 2)
    rows = DEC_SEQ * IDX_HEADS
    per_seq = lambda r, c: pl.BlockSpec((None, r, c), lambda s, pt: (s, 0, 0))

    def page(c, j):
        return pl.BlockSpec((None, PAGE_SIZE, c), lambda s, pt: (layer * n_pool + pt[s, j], 0, 0))

    in_specs = [per_seq(DEC_SEQ, ATT_W), per_seq(rows, IDX_DIM), per_seq(rows, 1),
                per_seq(SUBLANES, IDX_DIM), per_seq(SUBLANES, ATT_W), per_seq(SUBLANES, ATT_W)]
    in_specs += [page(IDX_DIM, j) for j in range(N_PAGES)]
    in_specs += [page(ATT_W, j) for j in range(N_PAGES)]
    in_specs += [page(ATT_W, j) for j in range(N_PAGES)]
    grid_spec = pltpu.PrefetchScalarGridSpec(
        num_scalar_prefetch=1, grid=(DEC_BATCH,), in_specs=in_specs,
        out_specs=per_seq(DEC_SEQ, ATT_W))
    return pl.pallas_call(
        _dsa_sample_kernel,
        grid_spec=grid_spec,
        out_shape=jax.ShapeDtypeStruct((DEC_BATCH, DEC_SEQ, ATT_W), f32),
        compiler_params=pltpu.CompilerParams(dimension_semantics=("arbitrary",),
                                             vmem_limit_bytes=VMEM_LIMIT),
        name="dsa_sample",
    )(page_table, q, qi, wi, ki_new, k_new, v_new,
      *([kidx_pages] * N_PAGES), *([k_pages] * N_PAGES), *([v_pages] * N_PAGES))


def _rope_tables(pos):
    half = ROT // 2
    inv = jnp.power(jnp.float32(ROPE_THETA), -jnp.arange(half, dtype=f32) * 2.0 / ROT)
    ang = pos.astype(f32)[:, None] * inv[None, :]
    cos, sin = jnp.cos(ang), jnp.sin(ang)
    t = pos.shape[0]
    rest = HEAD_DIM - ROT
    c64 = jnp.concatenate([cos, cos, jnp.ones((t, rest), f32)], axis=1)
    s1 = jnp.concatenate([jnp.zeros((t, half), f32), sin, jnp.zeros((t, rest), f32)], axis=1)
    s2 = jnp.concatenate([-sin, jnp.zeros((t, half + rest), f32)], axis=1)
    two = lambda a: jnp.concatenate([a, a], axis=1)
    return two(c64), two(s1), two(s2)


def _to_tm(a):
    return a.transpose(1, 0, 2).reshape(a.shape[1] * a.shape[0], a.shape[2])


def _from_tm(a, t=DEC_SEQ):
    return a.reshape(t, DEC_BATCH, a.shape[-1]).transpose(1, 0, 2)


def kernel(x_prompt, x_sample, state_b_conv, state_c_pool, cache_k, cache_v, cache_kidx, state_ffn_conv, page_table, w_in_even, a_ln_g, a_ln_b, a_ws, a_bs, b_conv_w, w_out_even, w_in_odd, c_w, c_scale, w_out_odd, ln_mix_g, ln_mix_b, ffn_w_up, ffn_conv_w, ffn_conv_b, ffn_w_down, ln_ffn_g, ln_ffn_b):
    row = lambda a: a.reshape(1, -1)
    xp = x_prompt
    xs = _to_tm(x_sample)
    outs = {}
    ffn_p, ffn_s = [], []

    def post(layer, xp, xs, cap, cbp, cas, cbs, wo):
        wts = (wo.astype(bf16), row(ln_mix_g[layer]), row(ln_mix_b[layer]), ffn_w_up[layer].astype(bf16),
               ffn_conv_w[layer], row(ffn_conv_b[layer]), ffn_w_down[layer].astype(bf16),
               row(ln_ffn_g[layer]), row(ln_ffn_b[layer]))
        xp, st_p = _post_prompt(xp, cap, cbp, *wts)
        xs, st_s = _post_sample(xs, cas, cbs, *wts, _to_tm(state_ffn_conv[layer]))
        ffn_p.append(st_p[:, SUBLANES - (CONV_W - 1):])
        ffn_s.append(_from_tm(st_s, CONV_W - 1))
        return xp, xs

    for layer in range(DEPTH):
        if layer % 2 == 0:
            e = layer // 2
            w = w_in_even[e].astype(bf16)
            lg, lb = row(a_ln_g[e]), row(a_ln_b[e])
            bias = jnp.repeat(a_bs[e].T, CHUNK, axis=1)
            wcoef = jnp.repeat(a_ws[e][:, :DEC_SEQ, :DEC_SEQ].transpose(1, 2, 0).reshape(DEC_SEQ * DEC_SEQ, A_GROUPS),
                               CHUNK, axis=1)
            a_p, b_p, bst_p = _even_pre_prompt(xp, w, lg, lb, a_ws[e], bias, b_conv_w[e])
            a_s, b_s, v_s, e_s = _even_pre_sample(xs, w, lg, lb, wcoef, bias, b_conv_w[e],
                                                  _to_tm(state_b_conv[e]))
            outs.setdefault("a_v_s", []).append(_from_tm(v_s))
            outs.setdefault("b_p", []).append(bst_p[:, SUBLANES - (CONV_W - 1):])
            outs.setdefault("b_s", []).append(_from_tm(e_s)[:, DEC_SEQ - (CONV_W - 1):])
            xp, xs = post(layer, xp, xs, a_p, b_p, a_s, b_s, w_out_even[e])
        else:
            o = layer // 2
            wm = w_in_odd[o][:, :O_KI].astype(bf16)
            wki = w_in_odd[o][:, O_KI:O_WI]
            wwi = w_in_odd[o][:, O_WI:]
            wt = jnp.concatenate([wki, wki, wwi, jnp.zeros((D_MODEL, LANES - IDX_HEADS), f32)],
                                 axis=1).astype(bf16)
            cwt = c_w[o].astype(bf16)
            cs = row(c_scale[o])
            tabs_p = _rope_tables(jnp.arange(SEQ, dtype=jnp.int32))
            pos_s = PAST_LEN + jnp.repeat(jnp.arange(DEC_SEQ, dtype=jnp.int32), DEC_BATCH)
            tabs_s = _rope_tables(pos_s)
            (k_p, v_p, ki_p, qb, kb, vb, qib, ki2b, wi_p, co_p, ph_p) = _odd_pre_prompt(
                xp, wm, wt, *tabs_p, cwt, cs)
            att_p = _dsa_prompt(qb, qib, wi_p, kb, vb, ki2b)
            (k_s, v_s2, ki_s, q_s, qi_s, wi_s, co_s, p_s) = _odd_pre_sample(
                xs, wm, wt, *tabs_s, cwt, cs, _to_tm(state_c_pool[o]))
            rows = DEC_SEQ * IDX_HEADS
            pad8 = lambda a: jnp.pad(_from_tm(a), ((0, 0), (0, SUBLANES - DEC_SEQ), (0, 0)))
            n_pool = cache_k.shape[1]
            att_s = _dsa_sample(
                page_table,
                _from_tm(q_s),
                _from_tm(qi_s).reshape(DEC_BATCH, rows, IDX_DIM),
                _from_tm(wi_s)[:, :, :IDX_HEADS].reshape(DEC_BATCH, rows, 1),
                pad8(ki_s), pad8(k_s), pad8(v_s2),
                cache_kidx.reshape(-1, PAGE_SIZE, IDX_DIM),
                cache_k.reshape(-1, PAGE_SIZE, ATT_W),
                cache_v.reshape(-1, PAGE_SIZE, ATT_W), o)
            del n_pool
            heads = lambda a: a.reshape(a.shape[0], a.shape[1], N_HEADS, HEAD_DIM)
            outs.setdefault("c_p", []).append(ph_p[:, 2 * SUBLANES - POOL_HIST:])
            outs.setdefault("c_s", []).append(
                jnp.concatenate([state_c_pool[o][:, DEC_SEQ:], _from_tm(p_s)], axis=1))
            outs.setdefault("k_p", []).append(heads(k_p))
            outs.setdefault("v_p", []).append(heads(v_p))
            outs.setdefault("ki_p", []).append(ki_p)
            outs.setdefault("k_s", []).append(heads(_from_tm(k_s)))
            outs.setdefault("v_s", []).append(heads(_from_tm(v_s2)))
            outs.setdefault("ki_s", []).append(_from_tm(ki_s))
            xp, xs = post(layer, xp, xs, co_p, att_p, co_s, _to_tm(att_s).astype(bf16), w_out_odd[o])

    st = lambda name: jnp.stack(outs[name])
    return (xp, _from_tm(xs), st("a_v_s"), st("b_p"), st("b_s"), st("c_p"), st("c_s"),
            st("k_p"), st("v_p"), st("ki_p"), st("k_s"), st("v_s"), st("ki_s"),
            jnp.stack(ffn_p), jnp.stack(ffn_s))
```

```python
import functools

import jax
import jax.numpy as jnp
from jax import lax
from jax.experimental import pallas as pl
from jax.experimental.pallas import tpu as pltpu

D_MODEL = 1024
BATCH = 4
SEQ = 4096
DEPTH = 2
DEC_BATCH = 128
DEC_SEQ = 4
PAST_LEN = 2048
PAGE_SIZE = 128
N_PAGES = PAST_LEN // PAGE_SIZE

A_W = D_MODEL // 2
B_W = D_MODEL // 2
C_W = D_MODEL // 2
ATT_W = D_MODEL // 2
CHUNK = 128
A_GROUPS = A_W // CHUNK
CONV_W = 3
POOL_WINDOWS = (2, 4, 8, 16)
C_GROUPS = len(POOL_WINDOWS)
C_GROUP_W = C_W // C_GROUPS
POOL_HIST = max(POOL_WINDOWS) - 1
HEAD_DIM = 64
N_HEADS = ATT_W // HEAD_DIM
IDX_HEADS = 8
IDX_DIM = 64
TOPK = 256
Q_BLOCK = 128
ROPE_THETA = 500000.0
ROT = HEAD_DIM // 4
D_FF = ((8 * D_MODEL // 3 + 127) // 128) * 128
ALPHA = (2.0 * DEPTH) ** 0.25
LN_EPS = 1e-5
ATT_SCALE = HEAD_DIM ** -0.5

E_U, E_V, E_B, E_C, E_H = 0, A_W, 2 * A_W, 2 * A_W + B_W, 2 * A_W + 2 * B_W
O_P, O_Q, O_K, O_V, O_QI = 0, C_W, C_W + ATT_W, C_W + 2 * ATT_W, C_W + 3 * ATT_W
O_KI = O_QI + IDX_HEADS * IDX_DIM
O_WI = O_KI + IDX_DIM

LANES = 128
SUBLANES = 8
VMEM_LIMIT = 56 * 1024 * 1024

TM = 512
FF_CHUNK = 256
KEY_CHUNK = 512
S_ROWS = DEC_SEQ * DEC_BATCH
INT_MIN = -2 ** 31
MIN16 = -2 ** 15
NEG_BIG = -1e30

bf16 = jnp.bfloat16
f32 = jnp.float32


def _ln(x, g, b):
    mu = jnp.mean(x, axis=-1, keepdims=True)
    xc = x - mu
    var = jnp.mean(xc * xc, axis=-1, keepdims=True)
    return xc * lax.rsqrt(var + LN_EPS) * g + b


def _gelu(x):
    c = (2.0 / jnp.pi) ** 0.5
    return x * (0.5 * (1.0 + jnp.tanh(c * (x + 0.044715 * (x * x * x)))))


def _dot(a, b):
    return jnp.dot(a, b, preferred_element_type=f32)


def _dot_nt(a, b):
    return lax.dot_general(a, b, (((1,), (1,)), ((), ())), preferred_element_type=f32)


def _prev_rows_prompt(cur, halo):
    ext = jnp.concatenate([halo, cur], axis=0)
    p1 = pltpu.roll(ext, 1, 0)[SUBLANES:]
    p2 = pltpu.roll(ext, 2, 0)[SUBLANES:]
    return p1, p2


def _prev_rows_sample(cur, hist):
    n = DEC_BATCH
    p1 = jnp.concatenate([hist[n:2 * n], cur[0:3 * n]], axis=0)
    p2 = jnp.concatenate([hist, cur[0:2 * n]], axis=0)
    return p1, p2


def _causal_conv(cur, p1, p2, w_ref, c0, width):
    return (cur * w_ref[2:3, c0:c0 + width] + p1 * w_ref[1:2, c0:c0 + width]
            + p2 * w_ref[0:1, c0:c0 + width])


def _even_pre_kernel(*refs, sample):
    if sample:
        (x_ref, w_ref, lg_ref, lb_ref, wcoef_ref, bias_ref, cw_ref, hist_ref,
         a_ref, b_ref, v_ref, e_ref) = refs
    else:
        (x_ref, w_ref, lg_ref, lb_ref, ws_ref, bias_ref, cw_ref,
         a_ref, b_ref, st_ref, halo_ref) = refs

        @pl.when(pl.program_id(1) == 0)
        def _():
            halo_ref[...] = jnp.zeros_like(halo_ref)

    xb = x_ref[...].astype(bf16)
    rows = xb.shape[0]

    def proj(c0):
        return _dot(xb, w_ref[:, c0:c0 + A_W])

    u = _gelu(proj(E_U))
    v = _ln(_gelu(proj(E_V)), lg_ref[...], lb_ref[...])
    if sample:
        v_ref[...] = v
        n = DEC_BATCH
        for t in range(DEC_SEQ):
            mix = bias_ref[t:t + 1, :]
            for s in range(t + 1):
                r = t * DEC_SEQ + s
                mix = mix + wcoef_ref[r:r + 1, :] * v[s * n:(s + 1) * n]
            a_ref[t * n:(t + 1) * n, :] = (u[t * n:(t + 1) * n] * mix).astype(bf16)
    else:
        vb = v.astype(bf16)
        ri = lax.broadcasted_iota(jnp.int32, (CHUNK, CHUNK), 0)
        ci = lax.broadcasted_iota(jnp.int32, (CHUNK, CHUNK), 1)
        for g in range(A_GROUPS):
            gs = slice(g * CHUNK, (g + 1) * CHUNK)
            wc = jnp.where(ri >= ci, ws_ref[g], 0.0).astype(bf16)
            for c in range(rows // CHUNK):
                rs = slice(c * CHUNK, (c + 1) * CHUNK)
                mix = _dot(wc, vb[rs, gs]) + bias_ref[:, gs]
                a_ref[rs, gs] = (u[rs, gs] * mix).astype(bf16)

    e = proj(E_C) * proj(E_H)
    if sample:
        e_ref[...] = e
        p1, p2 = _prev_rows_sample(e, hist_ref[...])
    else:
        p1, p2 = _prev_rows_prompt(e, halo_ref[...])
        halo_ref[...] = e[rows - SUBLANES:]
        st_ref[...] = e[rows - SUBLANES:]
    conv = _causal_conv(e, p1, p2, cw_ref, 0, B_W)
    b_ref[...] = (proj(E_B) * conv).astype(bf16)


def _full(shape):
    nd = len(shape)
    return pl.BlockSpec(shape, lambda *_: (0,) * nd)


def _even_pre_prompt(x, w, lg, lb, ws, bias, cw):
    nt = SEQ // TM
    row = lambda c: pl.BlockSpec((None, TM, c), lambda b, t: (b, t, 0))
    return pl.pallas_call(
        functools.partial(_even_pre_kernel, sample=False),
        grid=(BATCH, nt),
        in_specs=[row(D_MODEL), _full(w.shape), _full(lg.shape), _full(lb.shape), _full(ws.shape),
                  _full(bias.shape), _full(cw.shape)],
        out_specs=[row(A_W), row(B_W), pl.BlockSpec((None, SUBLANES, B_W), lambda b, t: (b, 0, 0))],
        out_shape=[jax.ShapeDtypeStruct((BATCH, SEQ, A_W), bf16),
                   jax.ShapeDtypeStruct((BATCH, SEQ, B_W), bf16),
                   jax.ShapeDtypeStruct((BATCH, SUBLANES, B_W), f32)],
        scratch_shapes=[pltpu.VMEM((SUBLANES, B_W), f32)],
        compiler_params=pltpu.CompilerParams(dimension_semantics=("arbitrary", "arbitrary"),
                                             vmem_limit_bytes=VMEM_LIMIT),
        name="even_pre_prompt",
    )(x, w, lg, lb, ws, bias, cw)


def _even_pre_sample(x, w, lg, lb, wcoef, bias, cw, hist):
    args = (x, w, lg, lb, wcoef, bias, cw, hist)
    return pl.pallas_call(
        functools.partial(_even_pre_kernel, sample=True),
        grid=(1,),
        in_specs=[_full(a.shape) for a in args],
        out_specs=[_full((S_ROWS, A_W)), _full((S_ROWS, B_W)), _full((S_ROWS, A_W)), _full((S_ROWS, B_W))],
        out_shape=[jax.ShapeDtypeStruct((S_ROWS, A_W), bf16),
                   jax.ShapeDtypeStruct((S_ROWS, B_W), bf16),
                   jax.ShapeDtypeStruct((S_ROWS, A_W), f32),
                   jax.ShapeDtypeStruct((S_ROWS, B_W), f32)],
        compiler_params=pltpu.CompilerParams(dimension_semantics=("arbitrary",),
                                             vmem_limit_bytes=VMEM_LIMIT),
        name="even_pre_sample",
    )(*args)


def _post_kernel(*refs, sample):
    if sample:
        (x_ref, ca_ref, cb_ref, wo_ref, g1_ref, b1_ref, wu_ref, cw_ref, cb2_ref, wd_ref, g2_ref, b2_ref,
         hist_ref, o_ref, st_ref, acc_ref) = refs
    else:
        (x_ref, ca_ref, cb_ref, wo_ref, g1_ref, b1_ref, wu_ref, cw_ref, cb2_ref, wd_ref, g2_ref, b2_ref,
         o_ref, st_ref, acc_ref, halo_ref) = refs

        @pl.when(pl.program_id(1) == 0)
        def _():
            halo_ref[...] = jnp.zeros_like(halo_ref)

    half = ca_ref.shape[-1]
    y = _dot(ca_ref[...], wo_ref[0:half, :]) + _dot(cb_ref[...], wo_ref[half:2 * half, :])
    x1 = _ln(ALPHA * x_ref[...] + y, g1_ref[...], b1_ref[...])
    x1b = x1.astype(bf16)
    rows = x1b.shape[0]

    def conv_half(c0):
        hu = _dot(x1b, wu_ref[:, c0:c0 + FF_CHUNK])
        if sample:
            p1, p2 = _prev_rows_sample(hu, hist_ref[:, c0:c0 + FF_CHUNK])
            st_ref[:, c0:c0 + FF_CHUNK] = hu[2 * DEC_BATCH:]
        else:
            p1, p2 = _prev_rows_prompt(hu, halo_ref[:, c0:c0 + FF_CHUNK])
            halo_ref[:, c0:c0 + FF_CHUNK] = hu[rows - SUBLANES:]
            st_ref[:, c0:c0 + FF_CHUNK] = hu[rows - SUBLANES:]
        return _causal_conv(hu, p1, p2, cw_ref, c0, FF_CHUNK) + cb2_ref[:, c0:c0 + FF_CHUNK]

    for c in range(D_FF // FF_CHUNK):
        c0 = c * FF_CHUNK
        hg = conv_half(c0)
        hv = conv_half(D_FF + c0)
        act = (hg * jax.nn.sigmoid(hg) * hv).astype(bf16)
        part = _dot(act, wd_ref[c0:c0 + FF_CHUNK, :])
        if c == 0:
            acc_ref[...] = part
        else:
            acc_ref[...] += part
    o_ref[...] = _ln(ALPHA * x1 + acc_ref[...], g2_ref[...], b2_ref[...])


def _post_prompt(x, ca, cb, wo, g1, b1, wu, cw, cb2, wd, g2, b2):
    nt = SEQ // TM
    row = lambda c: pl.BlockSpec((None, TM, c), lambda b, t: (b, t, 0))
    const = lambda a: pl.BlockSpec(a.shape, lambda b, t: (0,) * a.ndim, pipeline_mode=pl.Buffered(1))
    return pl.pallas_call(
        functools.partial(_post_kernel, sample=False),
        grid=(BATCH, nt),
        in_specs=[row(D_MODEL), row(ca.shape[-1]), row(cb.shape[-1])]
                 + [const(a) for a in (wo, g1, b1, wu, cw, cb2, wd, g2, b2)],
        out_specs=[row(D_MODEL), pl.BlockSpec((None, SUBLANES, 2 * D_FF), lambda b, t: (b, 0, 0))],
        out_shape=[jax.ShapeDtypeStruct((BATCH, SEQ, D_MODEL), f32),
                   jax.ShapeDtypeStruct((BATCH, SUBLANES, 2 * D_FF), f32)],
        scratch_shapes=[pltpu.VMEM((TM, D_MODEL), f32), pltpu.VMEM((SUBLANES, 2 * D_FF), f32)],
        compiler_params=pltpu.CompilerParams(dimension_semantics=("arbitrary", "arbitrary"),
                                             vmem_limit_bytes=VMEM_LIMIT),
        name="post_prompt",
    )(x, ca, cb, wo, g1, b1, wu, cw, cb2, wd, g2, b2)


def _post_sample(x, ca, cb, wo, g1, b1, wu, cw, cb2, wd, g2, b2, hist):
    args = (x, ca, cb, wo, g1, b1, wu, cw, cb2, wd, g2, b2, hist)
    return pl.pallas_call(
        functools.partial(_post_kernel, sample=True),
        grid=(1,),
        in_specs=[_full(a.shape) for a in args],
        out_specs=[_full((S_ROWS, D_MODEL)), _full((2 * DEC_BATCH, 2 * D_FF))],
        out_shape=[jax.ShapeDtypeStruct((S_ROWS, D_MODEL), f32),
                   jax.ShapeDtypeStruct((2 * DEC_BATCH, 2 * D_FF), f32)],
        scratch_shapes=[pltpu.VMEM((S_ROWS, D_MODEL), f32)],
        compiler_params=pltpu.CompilerParams(dimension_semantics=("arbitrary",),
                                             vmem_limit_bytes=VMEM_LIMIT),
        name="post_sample",
    )(*args)


def _rope(blk, cos, s1, s2):
    return blk * cos + pltpu.roll(blk, ROT // 2, 1) * s1 + pltpu.roll(blk, LANES - ROT // 2, 1) * s2


def _odd_pre_kernel(*refs, sample):
    if sample:
        (x_ref, wm_ref, wt_ref, cos_ref, s1_ref, s2_ref, cwt_ref, cs_ref, hist_ref,
         k_ref, v_ref, ki_ref, q_ref, qi_ref, wi_ref, co_ref, p_ref) = refs
    else:
        (x_ref, wm_ref, wt_ref, cos_ref, s1_ref, s2_ref, cwt_ref, cs_ref,
         k_ref, v_ref, ki_ref, qb_ref, kb_ref, vb_ref, qit_ref, kib_ref, wit_ref, co_ref, ph_ref,
         halo_ref) = refs

        @pl.when(pl.program_id(1) == 0)
        def _():
            halo_ref[...] = jnp.zeros_like(halo_ref)

    xb = x_ref[...].astype(bf16)
    rows = xb.shape[0]
    cos, s1, s2 = cos_ref[...], s1_ref[...], s2_ref[...]

    def proj(c0):
        return _dot(xb, wm_ref[:, c0:c0 + C_W])

    def rope_wide(h):
        return jnp.concatenate(
            [_rope(h[:, j * LANES:(j + 1) * LANES], cos, s1, s2) for j in range(h.shape[1] // LANES)], axis=1)

    q = rope_wide(proj(O_Q))
    k = rope_wide(proj(O_K))
    v = proj(O_V)
    qi = rope_wide(proj(O_QI))
    tail = _dot(xb, wt_ref[...])
    ki2 = _rope(tail[:, 0:LANES], cos, s1, s2)
    k_ref[...] = k
    v_ref[...] = v
    ki_ref[...] = ki2[:, 0:IDX_DIM]
    wi = tail[:, LANES:2 * LANES]
    if sample:
        q_ref[...] = q
        qi_ref[...] = qi
        wi_ref[...] = wi
    else:
        qb_ref[...] = (q * ATT_SCALE).astype(bf16)
        kb_ref[...] = k.astype(bf16)
        vb_ref[...] = v.astype(bf16)
        qit_ref[...] = qi.T.astype(bf16)
        kib_ref[...] = ki2[:, 0:IDX_DIM].astype(bf16)
        wit_ref[...] = wi.T[0:IDX_HEADS]

    p = proj(O_P)
    if sample:
        p_ref[...] = p
        n = DEC_BATCH
        hist = hist_ref[...]

        def slab(j, gs):
            if j < POOL_HIST:
                return hist[j * n:(j + 1) * n, gs]
            return p[(j - POOL_HIST) * n:(j - POOL_HIST + 1) * n, gs]

        for g, w in enumerate(POOL_WINDOWS):
            gs = slice(g * C_GROUP_W, (g + 1) * C_GROUP_W)
            for t in range(DEC_SEQ):
                win = slab(POOL_HIST + t, gs)
                for i in range(1, w):
                    win = win + slab(POOL_HIST + t - i, gs)
                pooled = win / float(w) - p[t * n:(t + 1) * n, gs]
                co = _dot(pooled.astype(bf16), cwt_ref[g]) * cs_ref[:, gs]
                co_ref[t * n:(t + 1) * n, gs] = co.astype(bf16)
    else:
        hrows = 2 * SUBLANES
        ext = jnp.concatenate([halo_ref[...], p], axis=0)
        pos = pl.program_id(1) * rows + lax.broadcasted_iota(jnp.int32, (rows, 1), 0)
        for g, w in enumerate(POOL_WINDOWS):
            gs = slice(g * C_GROUP_W, (g + 1) * C_GROUP_W)
            s = ext[:, gs]
            step = 1
            while step < w:
                s = s + pltpu.roll(s, step, 0)
                step *= 2
            cnt = jnp.minimum(w, pos + 1).astype(f32)
            pooled = s[hrows:] / cnt - p[:, gs]
            co = _dot(pooled.astype(bf16), cwt_ref[g]) * cs_ref[:, gs]
            co_ref[:, gs] = co.astype(bf16)
        halo_ref[...] = p[rows - hrows:]
        ph_ref[...] = p[rows - hrows:]


def _odd_pre_prompt(x, wm, wt, cos, s1, s2, cwt, cs):
    nt = SEQ // TM
    row = lambda c: pl.BlockSpec((None, TM, c), lambda b, t: (b, t, 0))
    col = lambda r: pl.BlockSpec((None, r, TM), lambda b, t: (b, 0, t))
    tab = pl.BlockSpec((TM, LANES), lambda b, t: (t, 0))
    sds = lambda c, dt: jax.ShapeDtypeStruct((BATCH, SEQ, c), dt)
    sdt = lambda r, dt: jax.ShapeDtypeStruct((BATCH, r, SEQ), dt)
    hrows = 2 * SUBLANES
    return pl.pallas_call(
        functools.partial(_odd_pre_kernel, sample=False),
        grid=(BATCH, nt),
        in_specs=[row(D_MODEL), _full(wm.shape), _full(wt.shape), tab, tab, tab, _full(cwt.shape),
                  _full(cs.shape)],
        out_specs=[row(ATT_W), row(ATT_W), row(IDX_DIM), row(ATT_W), row(ATT_W), row(ATT_W),
                   col(IDX_HEADS * IDX_DIM), row(IDX_DIM), col(IDX_HEADS), row(C_W),
                   pl.BlockSpec((None, hrows, C_W), lambda b, t: (b, 0, 0))],
        out_shape=[sds(ATT_W, f32), sds(ATT_W, f32), sds(IDX_DIM, f32), sds(ATT_W, bf16), sds(ATT_W, bf16),
                   sds(ATT_W, bf16), sdt(IDX_HEADS * IDX_DIM, bf16), sds(IDX_DIM, bf16), sdt(IDX_HEADS, f32),
                   sds(C_W, bf16), jax.ShapeDtypeStruct((BATCH, hrows, C_W), f32)],
        scratch_shapes=[pltpu.VMEM((hrows, C_W), f32)],
        compiler_params=pltpu.CompilerParams(dimension_semantics=("arbitrary", "arbitrary"),
                                             vmem_limit_bytes=VMEM_LIMIT),
        name="odd_pre_prompt",
    )(x, wm, wt, cos, s1, s2, cwt, cs)


def _odd_pre_sample(x, wm, wt, cos, s1, s2, cwt, cs, hist):
    args = (x, wm, wt, cos, s1, s2, cwt, cs, hist)
    sds = lambda c, dt: jax.ShapeDtypeStruct((S_ROWS, c), dt)
    outs = [sds(ATT_W, f32), sds(ATT_W, f32), sds(IDX_DIM, f32), sds(ATT_W, f32), sds(ATT_W, f32),
            sds(LANES, f32), sds(C_W, bf16), sds(C_W, f32)]
    return pl.pallas_call(
        functools.partial(_odd_pre_kernel, sample=True),
        grid=(1,),
        in_specs=[_full(a.shape) for a in args],
        out_specs=[_full(o.shape) for o in outs],
        out_shape=outs,
        compiler_params=pltpu.CompilerParams(dimension_semantics=("arbitrary",),
                                             vmem_limit_bytes=VMEM_LIMIT),
        name="odd_pre_sample",
    )(*args)


def _order_key(s):
    s = jnp.where(s == 0.0, 0.0, s)
    bits = lax.bitcast_convert_type(s, jnp.int32)
    return bits ^ ((bits >> 31) & 0x7FFFFFFF)


def _dsa_prompt_kernel(q_ref, qit_ref, wit_ref, k_ref, v_ref, ki_ref, o_ref,
                       hi_ref, lo_ref, lom_ref, m_ref, acc_ref):
    qb = pl.program_id(1)
    n_chunks = qb // (KEY_CHUNK // Q_BLOCK) + 1
    qpos = qb * Q_BLOCK + lax.broadcasted_iota(jnp.int32, (1, Q_BLOCK), 1)
    krow = lax.broadcasted_iota(jnp.int32, (KEY_CHUNK, 1), 0)
    krow_full = lax.broadcasted_iota(jnp.int32, (KEY_CHUNK, LANES), 0)
    kf = float(TOPK)
    i16 = jnp.int16

    qit = qit_ref[...]
    stat = jnp.concatenate([qit[h * IDX_DIM:(h + 1) * IDX_DIM, :] for h in range(IDX_HEADS)], axis=1)
    wit = wit_ref[...]

    def score_body(c, carry):
        ks = pl.multiple_of(c * KEY_CHUNK, KEY_CHUNK)
        d = _dot(ki_ref[pl.ds(ks, KEY_CHUNK), :], stat)
        s = jnp.zeros((KEY_CHUNK, Q_BLOCK), f32)
        for h in range(IDX_HEADS):
            s = s + wit[h:h + 1, :] * jnp.maximum(d[:, h * Q_BLOCK:(h + 1) * Q_BLOCK], 0.0)
        key = jnp.where((ks + krow) <= qpos, _order_key(s), INT_MIN)
        hi_ref[pl.ds(ks, KEY_CHUNK), :] = (key >> 16).astype(i16)
        lo_ref[pl.ds(ks, KEY_CHUNK), :] = ((key & 0xFFFF) + MIN16).astype(i16)
        return carry

    lax.fori_loop(0, n_chunks, score_body, 0)

    def count(pred):
        tile = 2 * SUBLANES

        def body(c, acc):
            ks = pl.multiple_of(c * KEY_CHUNK, KEY_CHUNK)
            mb = jnp.where(pred(ks), jnp.ones((), bf16), jnp.zeros((), bf16))
            m3 = mb.reshape(KEY_CHUNK // tile, tile, LANES)
            parts = [m3[i] for i in range(KEY_CHUNK // tile)]
            while len(parts) > 1:
                parts = [parts[i] + parts[i + 1] for i in range(0, len(parts), 2)]
            return acc + parts[0]

        acc = lax.fori_loop(0, n_chunks, body, jnp.zeros((tile, LANES), bf16))
        return jnp.sum(acc.astype(f32), axis=0, keepdims=True)

    def search16(ref, target):
        def cnt_ge(cand):
            c16 = cand.astype(i16)
            return count(lambda ks: ref[pl.ds(ks, KEY_CHUNK), :] >= c16)

        zero = jnp.zeros((1, LANES), jnp.int32)
        c0 = cnt_ge(zero)
        ok0 = c0 >= target
        t0 = jnp.where(ok0, zero, jnp.full((1, LANES), MIN16, jnp.int32))
        ct0 = jnp.where(ok0, c0, jnp.full((1, LANES), 1e9, f32))

        def body(i, st):
            t, ct = st
            cand = t | lax.shift_left(jnp.int32(1), 14 - i)
            c = cnt_ge(cand)
            ok = c >= target
            return jnp.where(ok, cand, t), jnp.where(ok, c, ct)

        return lax.fori_loop(0, 15, body, (t0, ct0))

    t_hi, _ = search16(hi_ref, jnp.full((1, LANES), kf, f32))
    thi16 = t_hi.astype(i16)
    need2 = kf - count(lambda ks: hi_ref[pl.ds(ks, KEY_CHUNK), :] > thi16)

    def mk_lom(c, carry):
        ks = pl.multiple_of(c * KEY_CHUNK, KEY_CHUNK)
        eq = hi_ref[pl.ds(ks, KEY_CHUNK), :] == thi16
        lom_ref[pl.ds(ks, KEY_CHUNK), :] = jnp.where(eq, lo_ref[pl.ds(ks, KEY_CHUNK), :],
                                                     jnp.full((), MIN16, i16))
        return carry

    lax.fori_loop(0, n_chunks, mk_lom, 0)
    t_lo, ct = search16(lom_ref, need2)
    tlo16 = t_lo.astype(i16)
    settled = jnp.logical_or(ct == need2, t_hi == MIN16)
    has_tie = jnp.max(jnp.where(settled, 0.0, 1.0)) > 0.0

    def tie_search():
        need3 = need2 - count(lambda ks: lom_ref[pl.ds(ks, KEY_CHUNK), :] > tlo16)
        big = jnp.full((), 32767, i16)

        def mk_pos(c, carry):
            ks = pl.multiple_of(c * KEY_CHUNK, KEY_CHUNK)
            pos = (ks + krow_full).astype(i16)
            tie_pos = jnp.where(lo_ref[pl.ds(ks, KEY_CHUNK), :] == tlo16, pos, big)
            lom_ref[pl.ds(ks, KEY_CHUNK), :] = jnp.where(hi_ref[pl.ds(ks, KEY_CHUNK), :] == thi16, tie_pos, big)
            return carry

        lax.fori_loop(0, n_chunks, mk_pos, 0)

        def jbit(i, j):
            cand = j | lax.shift_left(jnp.int32(1), 12 - i)
            c16 = cand.astype(i16)
            c = count(lambda ks: lom_ref[pl.ds(ks, KEY_CHUNK), :] < c16)
            return jnp.where(c <= need3, cand, j)

        return lax.fori_loop(0, 13, jbit, jnp.zeros((1, LANES), jnp.int32))

    jsel = lax.cond(has_tie, tie_search, lambda: jnp.full((1, LANES), 2 * SEQ, jnp.int32))
    jsel16 = jsel.astype(i16)

    low = lax.broadcasted_iota(jnp.int32, (1, LANES), 1) < HEAD_DIM
    m_ref[...] = jnp.full(m_ref.shape, NEG_BIG, f32)
    acc_ref[...] = jnp.zeros_like(acc_ref)
    q = q_ref[...]
    qpair = []
    for j in range(N_HEADS // 2):
        blk = q[:, j * LANES:(j + 1) * LANES]
        nil = jnp.zeros_like(blk)
        qpair.append(jnp.concatenate([jnp.where(low, blk, nil), jnp.where(low, nil, blk)], axis=0))
    one = jnp.ones((), bf16)
    open16 = jnp.zeros((), bf16)
    shut16 = jnp.full((), NEG_BIG, bf16)

    def att_body(c, carry):
        ks = pl.multiple_of(c * KEY_CHUNK, KEY_CHUNK)
        hk = hi_ref[pl.ds(ks, KEY_CHUNK), :]
        lk = lo_ref[pl.ds(ks, KEY_CHUNK), :]
        pos = (ks + krow_full).astype(i16)
        b = jnp.where(lk == tlo16, jnp.where(pos < jsel16, open16, shut16), shut16)
        b = jnp.where(hk == thi16, jnp.where(lk > tlo16, open16, b), shut16)
        b = jnp.where(hk > thi16, open16, b)
        b = jnp.where(hk == MIN16, shut16, b)
        bias = b.astype(f32).T

        def qk(j):
            return _dot_nt(qpair[j], k_ref[pl.ds(ks, KEY_CHUNK), j * LANES:(j + 1) * LANES])

        lg2 = qk(0)
        for j in range(N_HEADS // 2):
            nxt = qk(j + 1) if j + 1 < N_HEADS // 2 else None
            vblk = v_ref[pl.ds(ks, KEY_CHUNK), j * LANES:(j + 1) * LANES]
            for par in range(2):
                h = 2 * j + par
                lg = lg2[par * Q_BLOCK:(par + 1) * Q_BLOCK] + bias
                m_old = m_ref[h]
                m_new = jnp.maximum(m_old, jnp.max(lg, axis=1, keepdims=True))
                p = jnp.exp(lg - m_new).astype(bf16)
                a = jnp.exp(m_old - m_new)
                vaug = jnp.where(low, vblk, one) if par == 0 else jnp.where(low, one, vblk)
                acc_ref[h] = a * acc_ref[h] + _dot(p, vaug)
                m_ref[h] = m_new
            lg2 = nxt
        return carry

    lax.fori_loop(0, n_chunks, att_body, 0)
    for j in range(N_HEADS // 2):
        ev = acc_ref[2 * j]
        od = acc_ref[2 * j + 1]
        ev = ev / pltpu.roll(ev, HEAD_DIM, 1)
        od = od / pltpu.roll(od, HEAD_DIM, 1)
        o_ref[:, j * LANES:(j + 1) * LANES] = jnp.where(low, ev, od).astype(bf16)


def _dsa_prompt(qb, qit, wit, kb, vb, kib):
    nq = SEQ // Q_BLOCK
    qrow = lambda c: pl.BlockSpec((None, Q_BLOCK, c), lambda b, t: (b, t, 0))
    qcol = lambda r: pl.BlockSpec((None, r, Q_BLOCK), lambda b, t: (b, 0, t))
    seq = lambda c: pl.BlockSpec((None, SEQ, c), lambda b, t: (b, 0, 0))
    return pl.pallas_call(
        _dsa_prompt_kernel,
        grid=(BATCH, nq),
        in_specs=[qrow(ATT_W), qcol(IDX_HEADS * IDX_DIM), qcol(IDX_HEADS), seq(ATT_W), seq(ATT_W),
                  seq(IDX_DIM)],
        out_specs=qrow(ATT_W),
        out_shape=jax.ShapeDtypeStruct((BATCH, SEQ, ATT_W), bf16),
        scratch_shapes=[pltpu.VMEM((SEQ, Q_BLOCK), jnp.int16),
                        pltpu.VMEM((SEQ, Q_BLOCK), jnp.int16),
                        pltpu.VMEM((SEQ, Q_BLOCK), jnp.int16),
                        pltpu.VMEM((N_HEADS, Q_BLOCK, 1), f32),
                        pltpu.VMEM((N_HEADS, Q_BLOCK, LANES), f32)],
        compiler_params=pltpu.CompilerParams(dimension_semantics=("arbitrary", "arbitrary"),
                                             vmem_limit_bytes=VMEM_LIMIT),
        name="dsa_prompt",
    )(qb, qit, wit, kb, vb, kib)


S_KEYS = PAST_LEN + PAGE_SIZE
BITS_PER_PASS = 4


def _dsa_sample_kernel(pt_ref, q_ref, qi_ref, wi_ref, kin_ref, kn_ref, vn_ref, *rest):
    kidx_refs = rest[0:N_PAGES]
    k_refs = rest[N_PAGES:2 * N_PAGES]
    v_refs = rest[2 * N_PAGES:3 * N_PAGES]
    o_ref = rest[3 * N_PAGES]
    del pt_ref
    rows = DEC_SEQ * IDX_HEADS
    kf = float(TOPK)

    def pad_keys(x):
        return jnp.concatenate([x, jnp.zeros((PAGE_SIZE - x.shape[0], x.shape[1]), x.dtype)], axis=0)

    qi = qi_ref[...].astype(bf16)
    wi = wi_ref[...]

    def idx_score(d):
        r = jnp.maximum(d, 0.0) * wi
        return jnp.sum(r.reshape(DEC_SEQ, IDX_HEADS, PAGE_SIZE), axis=1)

    parts = [idx_score(_dot(qi, kidx_refs[j][...].astype(bf16))) for j in range(N_PAGES)]
    parts.append(idx_score(_dot_nt(qi, pad_keys(kin_ref[...]).astype(bf16))))
    s = jnp.concatenate(parts, axis=1)
    kpos = lax.broadcasted_iota(jnp.int32, (1, S_KEYS), 1)
    tq = lax.broadcasted_iota(jnp.int32, (DEC_SEQ, 1), 0)
    keys = jnp.where((kpos - PAST_LEN) <= tq, _order_key(s), INT_MIN)

    def count(pred):
        return jnp.sum(jnp.where(pred, 1.0, 0.0), axis=1, keepdims=True)

    zero = jnp.zeros((DEC_SEQ, 1), jnp.int32)
    c0 = count(keys >= zero)
    thr = jnp.where(c0 >= kf, zero, jnp.full((DEC_SEQ, 1), INT_MIN, jnp.int32))
    cthr = jnp.where(c0 >= kf, c0, 1e9)
    shift = 31
    while shift > 0:
        nb = min(BITS_PER_PASS, shift)
        shift -= nb
        num = zero
        for c in range(1, 2 ** nb):
            cnt = count(keys >= (thr | (c << shift)))
            ok = cnt >= kf
            num = num + ok.astype(jnp.int32)
            cthr = jnp.where(ok, cnt, cthr)
        thr = thr | lax.shift_left(num, shift)
    need = kf - count(keys > thr)
    settled = jnp.logical_or(cthr == kf, thr == INT_MIN)
    has_tie = jnp.max(jnp.where(settled, 0.0, 1.0)) > 0.0

    def tie_search():
        def jbit(i, j):
            cand = j | lax.shift_left(jnp.int32(1), 12 - i)
            c = count(jnp.logical_and(keys == thr, kpos < cand))
            return jnp.where(c <= need, cand, j)
        return lax.fori_loop(0, 13, jbit, zero)

    jsel = lax.cond(has_tie, tie_search, lambda: jnp.full((DEC_SEQ, 1), 2 * S_KEYS, jnp.int32))
    sel = jnp.logical_and(
        jnp.logical_or(keys > thr, jnp.logical_and(keys == thr, kpos < jsel)), keys != INT_MIN)
    bias4 = jnp.where(sel, 0.0, NEG_BIG)
    bias = jnp.broadcast_to(bias4[:, None, :], (DEC_SEQ, N_HEADS, S_KEYS)).reshape(rows, S_KEYS)

    q4 = q_ref[...] * ATT_SCALE
    q32 = jnp.broadcast_to(q4[:, None, :], (DEC_SEQ, N_HEADS, ATT_W)).reshape(rows, ATT_W)
    hrow = lax.broadcasted_iota(jnp.int32, (rows, ATT_W), 0) % N_HEADS
    hcol = lax.broadcasted_iota(jnp.int32, (rows, ATT_W), 1) // HEAD_DIM
    own = hrow == hcol
    qbd = jnp.where(own, q32, 0.0).astype(bf16)

    kn = pad_keys(kn_ref[...]).astype(bf16)
    vn = pad_keys(vn_ref[...]).astype(bf16)
    lg = jnp.concatenate([_dot(qbd, k_refs[j][...].astype(bf16)) for j in range(N_PAGES)]
                         + [_dot_nt(qbd, kn)], axis=1) + bias
    m = jnp.max(lg, axis=1, keepdims=True)
    p = jnp.exp(lg - m)
    l = jnp.sum(p, axis=1, keepdims=True)
    pb = p.astype(bf16)
    pv = _dot(pb[:, PAST_LEN:], vn)
    for j in range(N_PAGES):
        pv = pv + _dot_nt(pb[:, j * PAGE_SIZE:(j + 1) * PAGE_SIZE], v_refs[j][...].astype(bf16))
    out = jnp.where(own, pv, 0.0) / l
    o_ref[...] = jnp.sum(out.reshape(DEC_SEQ, N_HEADS, ATT_W), axis=1)


def _dsa_sample(page_table, q, qi, wi, ki_new, k_new, v_new, kidx_pages, k_pages, v_pages, layer):
    n_pool = kidx_pages.shape[0] // (DEPTH // 2)
    rows = DEC_SEQ * IDX_HEADS
    per_seq = lambda r, c: pl.BlockSpec((None, r, c), lambda s, pt: (s, 0, 0))

    def page(r, j):
        return pl.BlockSpec((None, r, PAGE_SIZE), lambda s, pt: (layer * n_pool + pt[s, j], 0, 0))

    in_specs = [per_seq(DEC_SEQ, ATT_W), per_seq(rows, IDX_DIM), per_seq(rows, 1),
                per_seq(SUBLANES, IDX_DIM), per_seq(SUBLANES, ATT_W), per_seq(SUBLANES, ATT_W)]
    in_specs += [page(IDX_DIM, j) for j in range(N_PAGES)]
    in_specs += [page(ATT_W, j) for j in range(N_PAGES)]
    in_specs += [page(ATT_W, j) for j in range(N_PAGES)]
    grid_spec = pltpu.PrefetchScalarGridSpec(
        num_scalar_prefetch=1, grid=(DEC_BATCH,), in_specs=in_specs,
        out_specs=per_seq(DEC_SEQ, ATT_W))
    return pl.pallas_call(
        _dsa_sample_kernel,
        grid_spec=grid_spec,
        out_shape=jax.ShapeDtypeStruct((DEC_BATCH, DEC_SEQ, ATT_W), f32),
        compiler_params=pltpu.CompilerParams(dimension_semantics=("arbitrary",),
                                             vmem_limit_bytes=VMEM_LIMIT),
        name="dsa_sample",
    )(page_table, q, qi, wi, ki_new, k_new, v_new,
      *([kidx_pages] * N_PAGES), *([k_pages] * N_PAGES), *([v_pages] * N_PAGES))


def _rope_tables(pos):
    half = ROT // 2
    inv = jnp.power(jnp.float32(ROPE_THETA), -jnp.arange(half, dtype=f32) * 2.0 / ROT)
    ang = pos.astype(f32)[:, None] * inv[None, :]
    cos, sin = jnp.cos(ang), jnp.sin(ang)
    t = pos.shape[0]
    rest = HEAD_DIM - ROT
    c64 = jnp.concatenate([cos, cos, jnp.ones((t, rest), f32)], axis=1)
    s1 = jnp.concatenate([jnp.zeros((t, half), f32), sin, jnp.zeros((t, rest), f32)], axis=1)
    s2 = jnp.concatenate([-sin, jnp.zeros((t, half + rest), f32)], axis=1)
    two = lambda a: jnp.concatenate([a, a], axis=1)
    return two(c64), two(s1), two(s2)


def _to_tm(a):
    return a.transpose(1, 0, 2).reshape(a.shape[1] * a.shape[0], a.shape[2])


def _from_tm(a, t=DEC_SEQ):
    return a.reshape(t, DEC_BATCH, a.shape[-1]).transpose(1, 0, 2)


def kernel(x_prompt, x_sample, state_b_conv, state_c_pool, cache_k, cache_v, cache_kidx, state_ffn_conv, page_table, w_in_even, a_ln_g, a_ln_b, a_ws, a_bs, b_conv_w, w_out_even, w_in_odd, c_w, c_scale, w_out_odd, ln_mix_g, ln_mix_b, ffn_w_up, ffn_conv_w, ffn_conv_b, ffn_w_down, ln_ffn_g, ln_ffn_b):
    row = lambda a: a.reshape(1, -1)
    xp = x_prompt
    xs = _to_tm(x_sample)
    outs = {}
    ffn_p, ffn_s = [], []

    def post(layer, xp, xs, cap, cbp, cas, cbs, wo):
        wts = (wo.astype(bf16), row(ln_mix_g[layer]), row(ln_mix_b[layer]), ffn_w_up[layer].astype(bf16),
               ffn_conv_w[layer], row(ffn_conv_b[layer]), ffn_w_down[layer].astype(bf16),
               row(ln_ffn_g[layer]), row(ln_ffn_b[layer]))
        xp, st_p = _post_prompt(xp, cap, cbp, *wts)
        xs, st_s = _post_sample(xs, cas, cbs, *wts, _to_tm(state_ffn_conv[layer]))
        ffn_p.append(st_p[:, SUBLANES - (CONV_W - 1):])
        ffn_s.append(_from_tm(st_s, CONV_W - 1))
        return xp, xs

    for layer in range(DEPTH):
        if layer % 2 == 0:
            e = layer // 2
            w = w_in_even[e].astype(bf16)
            lg, lb = row(a_ln_g[e]), row(a_ln_b[e])
            bias = jnp.repeat(a_bs[e].T, CHUNK, axis=1)
            wcoef = jnp.repeat(a_ws[e][:, :DEC_SEQ, :DEC_SEQ].transpose(1, 2, 0).reshape(DEC_SEQ * DEC_SEQ, A_GROUPS),
                               CHUNK, axis=1)
            a_p, b_p, bst_p = _even_pre_prompt(xp, w, lg, lb, a_ws[e], bias, b_conv_w[e])
            a_s, b_s, v_s, e_s = _even_pre_sample(xs, w, lg, lb, wcoef, bias, b_conv_w[e],
                                                  _to_tm(state_b_conv[e]))
            outs.setdefault("a_v_s", []).append(_from_tm(v_s))
            outs.setdefault("b_p", []).append(bst_p[:, SUBLANES - (CONV_W - 1):])
            outs.setdefault("b_s", []).append(_from_tm(e_s)[:, DEC_SEQ - (CONV_W - 1):])
            xp, xs = post(layer, xp, xs, a_p, b_p, a_s, b_s, w_out_even[e])
        else:
            o = layer // 2
            wm = w_in_odd[o][:, :O_KI].astype(bf16)
            wki = w_in_odd[o][:, O_KI:O_WI]
            wwi = w_in_odd[o][:, O_WI:]
            wt = jnp.concatenate([wki, wki, wwi, jnp.zeros((D_MODEL, LANES - IDX_HEADS), f32)],
                                 axis=1).astype(bf16)
            cwt = c_w[o].astype(bf16)
            cs = row(c_scale[o])
            tabs_p = _rope_tables(jnp.arange(SEQ, dtype=jnp.int32))
            pos_s = PAST_LEN + jnp.repeat(jnp.arange(DEC_SEQ, dtype=jnp.int32), DEC_BATCH)
            tabs_s = _rope_tables(pos_s)
            (k_p, v_p, ki_p, qb, kb, vb, qit, kib, wit, co_p, ph_p) = _odd_pre_prompt(
                xp, wm, wt, *tabs_p, cwt, cs)
            att_p = _dsa_prompt(qb, qit, wit, kb, vb, kib)
            (k_s, v_s2, ki_s, q_s, qi_s, wi_s, co_s, p_s) = _odd_pre_sample(
                xs, wm, wt, *tabs_s, cwt, cs, _to_tm(state_c_pool[o]))
            rows = DEC_SEQ * IDX_HEADS
            pad8 = lambda a: jnp.pad(_from_tm(a), ((0, 0), (0, SUBLANES - DEC_SEQ), (0, 0)))
            page_t = lambda c, w: jnp.moveaxis(c, 2, -1).reshape(-1, w, PAGE_SIZE)
            att_s = _dsa_sample(
                page_table,
                _from_tm(q_s),
                _from_tm(qi_s).reshape(DEC_BATCH, rows, IDX_DIM),
                _from_tm(wi_s)[:, :, :IDX_HEADS].reshape(DEC_BATCH, rows, 1),
                pad8(ki_s), pad8(k_s), pad8(v_s2),
                page_t(cache_kidx, IDX_DIM), page_t(cache_k, ATT_W), page_t(cache_v, ATT_W), o)
            heads = lambda a: a.reshape(a.shape[0], a.shape[1], N_HEADS, HEAD_DIM)
            outs.setdefault("c_p", []).append(ph_p[:, 2 * SUBLANES - POOL_HIST:])
            outs.setdefault("c_s", []).append(
                jnp.concatenate([state_c_pool[o][:, DEC_SEQ:], _from_tm(p_s)], axis=1))
            outs.setdefault("k_p", []).append(heads(k_p))
            outs.setdefault("v_p", []).append(heads(v_p))
            outs.setdefault("ki_p", []).append(ki_p)
            outs.setdefault("k_s", []).append(heads(_from_tm(k_s)))
            outs.setdefault("v_s", []).append(heads(_from_tm(v_s2)))
            outs.setdefault("ki_s", []).append(_from_tm(ki_s))
            xp, xs = post(layer, xp, xs, co_p, att_p, co_s, _to_tm(att_s).astype(bf16), w_out_odd[o])

    st = lambda name: jnp.stack(outs[name])
    return (xp, _from_tm(xs), st("a_v_s"), st("b_p"), st("b_s"), st("c_p"), st("c_s"),
            st("k_p"), st("v_p"), st("ki_p"), st("k_s"), st("v_s"), st("ki_s"),
            jnp.stack(ffn_p), jnp.stack(ffn_s))
```

```python
import functools

import jax
import jax.numpy as jnp
from jax import lax
from jax.experimental import pallas as pl
from jax.experimental.pallas import tpu as pltpu

D_MODEL = 1024
BATCH = 4
SEQ = 4096
DEPTH = 2
DEC_BATCH = 128
DEC_SEQ = 4
PAST_LEN = 2048
PAGE_SIZE = 128
N_PAGES = PAST_LEN // PAGE_SIZE

A_W = D_MODEL // 2
B_W = D_MODEL // 2
C_W = D_MODEL // 2
ATT_W = D_MODEL // 2
CHUNK = 128
A_GROUPS = A_W // CHUNK
CONV_W = 3
POOL_WINDOWS = (2, 4, 8, 16)
C_GROUPS = len(POOL_WINDOWS)
C_GROUP_W = C_W // C_GROUPS
POOL_HIST = max(POOL_WINDOWS) - 1
HEAD_DIM = 64
N_HEADS = ATT_W // HEAD_DIM
IDX_HEADS = 8
IDX_DIM = 64
TOPK = 256
Q_BLOCK = 128
ROPE_THETA = 500000.0
ROT = HEAD_DIM // 4
D_FF = ((8 * D_MODEL // 3 + 127) // 128) * 128
ALPHA = (2.0 * DEPTH) ** 0.25
LN_EPS = 1e-5
ATT_SCALE = HEAD_DIM ** -0.5
LOG2E = 1.4426950408889634

E_U, E_V, E_B, E_C, E_H = 0, A_W, 2 * A_W, 2 * A_W + B_W, 2 * A_W + 2 * B_W
O_P, O_Q, O_K, O_V, O_QI = 0, C_W, C_W + ATT_W, C_W + 2 * ATT_W, C_W + 3 * ATT_W
O_KI = O_QI + IDX_HEADS * IDX_DIM
O_WI = O_KI + IDX_DIM

LANES = 128
SUBLANES = 8
VMEM_LIMIT = 56 * 1024 * 1024

TM = 512
TM_POST = 256
FF_CHUNK = 256
KEY_CHUNK = 512
COUNT_CHUNK = 1024
S_ROWS = DEC_SEQ * DEC_BATCH
INT_MIN = -2 ** 31
MIN16 = -2 ** 15
NEG_BIG = -1e30

bf16 = jnp.bfloat16
f32 = jnp.float32


def _ln(x, g, b):
    mu = jnp.mean(x, axis=-1, keepdims=True)
    xc = x - mu
    var = jnp.mean(xc * xc, axis=-1, keepdims=True)
    return xc * lax.rsqrt(var + LN_EPS) * g + b


def _gelu(x):
    c = (2.0 / jnp.pi) ** 0.5
    return x * (0.5 * (1.0 + jnp.tanh(c * (x + 0.044715 * (x * x * x)))))


def _dot(a, b):
    return jnp.dot(a, b, preferred_element_type=f32)


def _dot_nt(a, b):
    return lax.dot_general(a, b, (((1,), (1,)), ((), ())), preferred_element_type=f32)


def _prev_rows_prompt(cur, halo):
    ext = jnp.concatenate([halo, cur], axis=0)
    p1 = pltpu.roll(ext, 1, 0)[SUBLANES:]
    p2 = pltpu.roll(ext, 2, 0)[SUBLANES:]
    return p1, p2


def _prev_rows_sample(cur, hist):
    n = DEC_BATCH
    p1 = jnp.concatenate([hist[n:2 * n], cur[0:3 * n]], axis=0)
    p2 = jnp.concatenate([hist, cur[0:2 * n]], axis=0)
    return p1, p2


def _causal_conv(cur, p1, p2, w_ref, c0, width):
    return (cur * w_ref[2:3, c0:c0 + width] + p1 * w_ref[1:2, c0:c0 + width]
            + p2 * w_ref[0:1, c0:c0 + width])


def _even_pre_kernel(*refs, sample):
    if sample:
        (x_ref, w_ref, lg_ref, lb_ref, wcoef_ref, bias_ref, cw_ref, hist_ref,
         a_ref, b_ref, v_ref, e_ref) = refs
    else:
        (x_ref, w_ref, lg_ref, lb_ref, ws_ref, bias_ref, cw_ref,
         a_ref, b_ref, st_ref, halo_ref) = refs

        @pl.when(pl.program_id(1) == 0)
        def _():
            halo_ref[...] = jnp.zeros_like(halo_ref)

    xb = x_ref[...].astype(bf16)
    rows = xb.shape[0]

    def proj(c0):
        return _dot(xb, w_ref[:, c0:c0 + A_W])

    u = _gelu(proj(E_U))
    v = _ln(_gelu(proj(E_V)), lg_ref[...], lb_ref[...])
    if sample:
        v_ref[...] = v
        n = DEC_BATCH
        for t in range(DEC_SEQ):
            mix = bias_ref[t:t + 1, :]
            for s in range(t + 1):
                r = t * DEC_SEQ + s
                mix = mix + wcoef_ref[r:r + 1, :] * v[s * n:(s + 1) * n]
            a_ref[t * n:(t + 1) * n, :] = (u[t * n:(t + 1) * n] * mix).astype(bf16)
    else:
        vb = v.astype(bf16)
        ri = lax.broadcasted_iota(jnp.int32, (CHUNK, CHUNK), 0)
        ci = lax.broadcasted_iota(jnp.int32, (CHUNK, CHUNK), 1)
        for g in range(A_GROUPS):
            gs = slice(g * CHUNK, (g + 1) * CHUNK)
            wc = jnp.where(ri >= ci, ws_ref[g], 0.0).astype(bf16)
            for c in range(rows // CHUNK):
                rs = slice(c * CHUNK, (c + 1) * CHUNK)
                mix = _dot(wc, vb[rs, gs]) + bias_ref[:, gs]
                a_ref[rs, gs] = (u[rs, gs] * mix).astype(bf16)

    e = proj(E_C) * proj(E_H)
    if sample:
        e_ref[...] = e
        p1, p2 = _prev_rows_sample(e, hist_ref[...])
    else:
        p1, p2 = _prev_rows_prompt(e, halo_ref[...])
        halo_ref[...] = e[rows - SUBLANES:]
        st_ref[...] = e[rows - SUBLANES:]
    conv = _causal_conv(e, p1, p2, cw_ref, 0, B_W)
    b_ref[...] = (proj(E_B) * conv).astype(bf16)


def _full(shape):
    nd = len(shape)
    return pl.BlockSpec(shape, lambda *_: (0,) * nd)


def _even_pre_prompt(x, w, lg, lb, ws, bias, cw):
    nt = SEQ // TM
    row = lambda c: pl.BlockSpec((None, TM, c), lambda b, t: (b, t, 0))
    return pl.pallas_call(
        functools.partial(_even_pre_kernel, sample=False),
        grid=(BATCH, nt),
        in_specs=[row(D_MODEL), _full(w.shape), _full(lg.shape), _full(lb.shape), _full(ws.shape),
                  _full(bias.shape), _full(cw.shape)],
        out_specs=[row(A_W), row(B_W), pl.BlockSpec((None, SUBLANES, B_W), lambda b, t: (b, 0, 0))],
        out_shape=[jax.ShapeDtypeStruct((BATCH, SEQ, A_W), bf16),
                   jax.ShapeDtypeStruct((BATCH, SEQ, B_W), bf16),
                   jax.ShapeDtypeStruct((BATCH, SUBLANES, B_W), f32)],
        scratch_shapes=[pltpu.VMEM((SUBLANES, B_W), f32)],
        compiler_params=pltpu.CompilerParams(dimension_semantics=("arbitrary", "arbitrary"),
                                             vmem_limit_bytes=VMEM_LIMIT),
        name="even_pre_prompt",
    )(x, w, lg, lb, ws, bias, cw)


def _even_pre_sample(x, w, lg, lb, wcoef, bias, cw, hist):
    args = (x, w, lg, lb, wcoef, bias, cw, hist)
    return pl.pallas_call(
        functools.partial(_even_pre_kernel, sample=True),
        grid=(1,),
        in_specs=[_full(a.shape) for a in args],
        out_specs=[_full((S_ROWS, A_W)), _full((S_ROWS, B_W)), _full((S_ROWS, A_W)), _full((S_ROWS, B_W))],
        out_shape=[jax.ShapeDtypeStruct((S_ROWS, A_W), bf16),
                   jax.ShapeDtypeStruct((S_ROWS, B_W), bf16),
                   jax.ShapeDtypeStruct((S_ROWS, A_W), f32),
                   jax.ShapeDtypeStruct((S_ROWS, B_W), f32)],
        compiler_params=pltpu.CompilerParams(dimension_semantics=("arbitrary",),
                                             vmem_limit_bytes=VMEM_LIMIT),
        name="even_pre_sample",
    )(*args)


def _post_kernel(*refs, sample):
    if sample:
        (x_ref, ca_ref, cb_ref, wo_ref, g1_ref, b1_ref, wu_ref, cw_ref, cb2_ref, wd_ref, g2_ref, b2_ref,
         hist_ref, o_ref, st_ref, acc_ref) = refs
    else:
        (x_ref, ca_ref, cb_ref, wo_ref, g1_ref, b1_ref, wu_ref, cw_ref, cb2_ref, wd_ref, g2_ref, b2_ref,
         o_ref, st_ref, acc_ref, halo_ref) = refs

        @pl.when(pl.program_id(1) == 0)
        def _():
            halo_ref[...] = jnp.zeros_like(halo_ref)

    half = ca_ref.shape[-1]
    y = _dot(ca_ref[...], wo_ref[0:half, :]) + _dot(cb_ref[...], wo_ref[half:2 * half, :])
    x1 = _ln(ALPHA * x_ref[...] + y, g1_ref[...], b1_ref[...])
    x1b = x1.astype(bf16)
    rows = x1b.shape[0]

    def up(c0):
        return (_dot(x1b, wu_ref[:, c0:c0 + FF_CHUNK]), _dot(x1b, wu_ref[:, D_FF + c0:D_FF + c0 + FF_CHUNK]))

    def conv_half(hu, c0):
        if sample:
            p1, p2 = _prev_rows_sample(hu, hist_ref[:, c0:c0 + FF_CHUNK])
            st_ref[:, c0:c0 + FF_CHUNK] = hu[2 * DEC_BATCH:]
        else:
            p1, p2 = _prev_rows_prompt(hu, halo_ref[:, c0:c0 + FF_CHUNK])
            halo_ref[:, c0:c0 + FF_CHUNK] = hu[rows - SUBLANES:]
            st_ref[:, c0:c0 + FF_CHUNK] = hu[rows - SUBLANES:]
        return _causal_conv(hu, p1, p2, cw_ref, c0, FF_CHUNK) + cb2_ref[:, c0:c0 + FF_CHUNK]

    n_ff = D_FF // FF_CHUNK
    nxt = up(0)
    for c in range(n_ff):
        c0 = c * FF_CHUNK
        cur = nxt
        if c + 1 < n_ff:
            nxt = up(c0 + FF_CHUNK)
        hg = conv_half(cur[0], c0)
        hv = conv_half(cur[1], D_FF + c0)
        act = (hg * jax.nn.sigmoid(hg) * hv).astype(bf16)
        part = _dot(act, wd_ref[c0:c0 + FF_CHUNK, :])
        if c == 0:
            acc_ref[...] = part
        else:
            acc_ref[...] += part
    o_ref[...] = _ln(ALPHA * x1 + acc_ref[...], g2_ref[...], b2_ref[...])


def _post_prompt(x, ca, cb, wo, g1, b1, wu, cw, cb2, wd, g2, b2):
    nt = SEQ // TM_POST
    row = lambda c: pl.BlockSpec((None, TM_POST, c), lambda b, t: (b, t, 0))
    const = lambda a: pl.BlockSpec(a.shape, lambda b, t: (0,) * a.ndim, pipeline_mode=pl.Buffered(1))
    return pl.pallas_call(
        functools.partial(_post_kernel, sample=False),
        grid=(BATCH, nt),
        in_specs=[row(D_MODEL), row(ca.shape[-1]), row(cb.shape[-1])]
                 + [const(a) for a in (wo, g1, b1, wu, cw, cb2, wd, g2, b2)],
        out_specs=[row(D_MODEL), pl.BlockSpec((None, SUBLANES, 2 * D_FF), lambda b, t: (b, 0, 0))],
        out_shape=[jax.ShapeDtypeStruct((BATCH, SEQ, D_MODEL), f32),
                   jax.ShapeDtypeStruct((BATCH, SUBLANES, 2 * D_FF), f32)],
        scratch_shapes=[pltpu.VMEM((TM_POST, D_MODEL), f32), pltpu.VMEM((SUBLANES, 2 * D_FF), f32)],
        compiler_params=pltpu.CompilerParams(dimension_semantics=("arbitrary", "arbitrary"),
                                             vmem_limit_bytes=VMEM_LIMIT),
        name="post_prompt",
    )(x, ca, cb, wo, g1, b1, wu, cw, cb2, wd, g2, b2)


def _post_sample(x, ca, cb, wo, g1, b1, wu, cw, cb2, wd, g2, b2, hist):
    args = (x, ca, cb, wo, g1, b1, wu, cw, cb2, wd, g2, b2, hist)
    return pl.pallas_call(
        functools.partial(_post_kernel, sample=True),
        grid=(1,),
        in_specs=[_full(a.shape) for a in args],
        out_specs=[_full((S_ROWS, D_MODEL)), _full((2 * DEC_BATCH, 2 * D_FF))],
        out_shape=[jax.ShapeDtypeStruct((S_ROWS, D_MODEL), f32),
                   jax.ShapeDtypeStruct((2 * DEC_BATCH, 2 * D_FF), f32)],
        scratch_shapes=[pltpu.VMEM((S_ROWS, D_MODEL), f32)],
        compiler_params=pltpu.CompilerParams(dimension_semantics=("arbitrary",),
                                             vmem_limit_bytes=VMEM_LIMIT),
        name="post_sample",
    )(*args)


def _rope(blk, cos, s1, s2):
    return blk * cos + pltpu.roll(blk, ROT // 2, 1) * s1 + pltpu.roll(blk, LANES - ROT // 2, 1) * s2


def _odd_pre_kernel(*refs, sample):
    if sample:
        (x_ref, wm_ref, wt_ref, cos_ref, s1_ref, s2_ref, cwt_ref, cs_ref, hist_ref,
         k_ref, v_ref, ki_ref, q_ref, qi_ref, wi_ref, co_ref, p_ref) = refs
    else:
        (x_ref, wm_ref, wt_ref, cos_ref, s1_ref, s2_ref, cwt_ref, cs_ref,
         k_ref, v_ref, ki_ref, qb_ref, kb_ref, va_ref, qit_ref, kib_ref, wit_ref, co_ref, ph_ref,
         halo_ref) = refs

        @pl.when(pl.program_id(1) == 0)
        def _():
            halo_ref[...] = jnp.zeros_like(halo_ref)

    xb = x_ref[...].astype(bf16)
    rows = xb.shape[0]
    cos, s1, s2 = cos_ref[...], s1_ref[...], s2_ref[...]

    def proj(c0):
        return _dot(xb, wm_ref[:, c0:c0 + C_W])

    def rope_wide(h):
        return jnp.concatenate(
            [_rope(h[:, j * LANES:(j + 1) * LANES], cos, s1, s2) for j in range(h.shape[1] // LANES)], axis=1)

    q = rope_wide(proj(O_Q))
    k = rope_wide(proj(O_K))
    v = proj(O_V)
    qi = rope_wide(proj(O_QI))
    tail = _dot(xb, wt_ref[...])
    ki2 = _rope(tail[:, 0:LANES], cos, s1, s2)
    k_ref[...] = k
    v_ref[...] = v
    ki_ref[...] = ki2[:, 0:IDX_DIM]
    wi = tail[:, LANES:2 * LANES]
    if sample:
        q_ref[...] = q
        qi_ref[...] = qi
        wi_ref[...] = wi
    else:
        qb_ref[...] = (q * (ATT_SCALE * LOG2E)).astype(bf16)
        kb_ref[...] = k.astype(bf16)
        low = lax.broadcasted_iota(jnp.int32, (1, LANES), 1) < HEAD_DIM
        for j in range(N_HEADS // 2):
            vj = v[:, j * LANES:(j + 1) * LANES]
            va_ref[:, 2 * j * LANES:(2 * j + 1) * LANES] = jnp.where(low, vj, 1.0).astype(bf16)
            va_ref[:, (2 * j + 1) * LANES:(2 * j + 2) * LANES] = jnp.where(low, 1.0, vj).astype(bf16)
        qit_ref[...] = qi.T.astype(bf16)
        kib_ref[...] = ki2[:, 0:IDX_DIM].astype(bf16)
        wit_ref[...] = wi.T[0:IDX_HEADS]

    p = proj(O_P)
    if sample:
        p_ref[...] = p
        n = DEC_BATCH
        hist = hist_ref[...]

        def slab(j, gs):
            if j < POOL_HIST:
                return hist[j * n:(j + 1) * n, gs]
            return p[(j - POOL_HIST) * n:(j - POOL_HIST + 1) * n, gs]

        for g, w in enumerate(POOL_WINDOWS):
            gs = slice(g * C_GROUP_W, (g + 1) * C_GROUP_W)
            for t in range(DEC_SEQ):
                win = slab(POOL_HIST + t, gs)
                for i in range(1, w):
                    win = win + slab(POOL_HIST + t - i, gs)
                pooled = win / float(w) - p[t * n:(t + 1) * n, gs]
                co = _dot(pooled.astype(bf16), cwt_ref[g]) * cs_ref[:, gs]
                co_ref[t * n:(t + 1) * n, gs] = co.astype(bf16)
    else:
        hrows = 2 * SUBLANES
        ext = jnp.concatenate([halo_ref[...], p], axis=0)
        pos = pl.program_id(1) * rows + lax.broadcasted_iota(jnp.int32, (rows, 1), 0)
        for g, w in enumerate(POOL_WINDOWS):
            gs = slice(g * C_GROUP_W, (g + 1) * C_GROUP_W)
            s = ext[:, gs]
            step = 1
            while step < w:
                s = s + pltpu.roll(s, step, 0)
                step *= 2
            cnt = jnp.minimum(w, pos + 1).astype(f32)
            pooled = s[hrows:] / cnt - p[:, gs]
            co = _dot(pooled.astype(bf16), cwt_ref[g]) * cs_ref[:, gs]
            co_ref[:, gs] = co.astype(bf16)
        halo_ref[...] = p[rows - hrows:]
        ph_ref[...] = p[rows - hrows:]


def _odd_pre_prompt(x, wm, wt, cos, s1, s2, cwt, cs):
    nt = SEQ // TM
    row = lambda c: pl.BlockSpec((None, TM, c), lambda b, t: (b, t, 0))
    col = lambda r: pl.BlockSpec((None, r, TM), lambda b, t: (b, 0, t))
    tab = pl.BlockSpec((TM, LANES), lambda b, t: (t, 0))
    sds = lambda c, dt: jax.ShapeDtypeStruct((BATCH, SEQ, c), dt)
    sdt = lambda r, dt: jax.ShapeDtypeStruct((BATCH, r, SEQ), dt)
    hrows = 2 * SUBLANES
    return pl.pallas_call(
        functools.partial(_odd_pre_kernel, sample=False),
        grid=(BATCH, nt),
        in_specs=[row(D_MODEL), _full(wm.shape), _full(wt.shape), tab, tab, tab, _full(cwt.shape),
                  _full(cs.shape)],
        out_specs=[row(ATT_W), row(ATT_W), row(IDX_DIM), row(ATT_W), row(ATT_W), row(2 * ATT_W),
                   col(IDX_HEADS * IDX_DIM), row(IDX_DIM), col(IDX_HEADS), row(C_W),
                   pl.BlockSpec((None, hrows, C_W), lambda b, t: (b, 0, 0))],
        out_shape=[sds(ATT_W, f32), sds(ATT_W, f32), sds(IDX_DIM, f32), sds(ATT_W, bf16), sds(ATT_W, bf16),
                   sds(2 * ATT_W, bf16), sdt(IDX_HEADS * IDX_DIM, bf16), sds(IDX_DIM, bf16), sdt(IDX_HEADS, f32),
                   sds(C_W, bf16), jax.ShapeDtypeStruct((BATCH, hrows, C_W), f32)],
        scratch_shapes=[pltpu.VMEM((hrows, C_W), f32)],
        compiler_params=pltpu.CompilerParams(dimension_semantics=("arbitrary", "arbitrary"),
                                             vmem_limit_bytes=VMEM_LIMIT),
        name="odd_pre_prompt",
    )(x, wm, wt, cos, s1, s2, cwt, cs)


def _odd_pre_sample(x, wm, wt, cos, s1, s2, cwt, cs, hist):
    args = (x, wm, wt, cos, s1, s2, cwt, cs, hist)
    sds = lambda c, dt: jax.ShapeDtypeStruct((S_ROWS, c), dt)
    outs = [sds(ATT_W, f32), sds(ATT_W, f32), sds(IDX_DIM, f32), sds(ATT_W, f32), sds(ATT_W, f32),
            sds(LANES, f32), sds(C_W, bf16), sds(C_W, f32)]
    return pl.pallas_call(
        functools.partial(_odd_pre_kernel, sample=True),
        grid=(1,),
        in_specs=[_full(a.shape) for a in args],
        out_specs=[_full(o.shape) for o in outs],
        out_shape=outs,
        compiler_params=pltpu.CompilerParams(dimension_semantics=("arbitrary",),
                                             vmem_limit_bytes=VMEM_LIMIT),
        name="odd_pre_sample",
    )(*args)


def _order_key(s):
    s = jnp.where(s == 0.0, 0.0, s)
    bits = lax.bitcast_convert_type(s, jnp.int32)
    return bits ^ ((bits >> 31) & 0x7FFFFFFF)


def _dsa_prompt_kernel(q_ref, qit_ref, wit_ref, k_ref, va_ref, ki_ref, o_ref,
                       hi_ref, lo_ref, lom_ref, m_ref, acc_ref, lga_ref, lgb_ref):
    qb = pl.program_id(1)
    n_chunks = qb // (KEY_CHUNK // Q_BLOCK) + 1
    qpos = qb * Q_BLOCK + lax.broadcasted_iota(jnp.int32, (1, Q_BLOCK), 1)
    krow = lax.broadcasted_iota(jnp.int32, (KEY_CHUNK, 1), 0)
    krow_full = lax.broadcasted_iota(jnp.int32, (KEY_CHUNK, LANES), 0)
    kf = float(TOPK)
    i16 = jnp.int16

    @pl.when(qb == 0)
    def _():
        for ref in (hi_ref, lo_ref, lom_ref):
            ref[...] = jnp.full(ref.shape, MIN16, i16)

    qit = qit_ref[...]
    stat = jnp.concatenate([qit[h * IDX_DIM:(h + 1) * IDX_DIM, :] for h in range(IDX_HEADS)], axis=1)
    wit = wit_ref[...]

    def score_body(c, carry):
        ks = pl.multiple_of(c * KEY_CHUNK, KEY_CHUNK)
        d = _dot(ki_ref[pl.ds(ks, KEY_CHUNK), :], stat)
        s = jnp.zeros((KEY_CHUNK, Q_BLOCK), f32)
        for h in range(IDX_HEADS):
            s = s + wit[h:h + 1, :] * jnp.maximum(d[:, h * Q_BLOCK:(h + 1) * Q_BLOCK], 0.0)
        key = jnp.where((ks + krow) <= qpos, _order_key(s), INT_MIN)
        hi_ref[pl.ds(ks, KEY_CHUNK), :] = (key >> 16).astype(i16)
        lo_ref[pl.ds(ks, KEY_CHUNK), :] = ((key & 0xFFFF) + MIN16).astype(i16)
        return carry

    lax.fori_loop(0, n_chunks, score_body, 0)

    def count(pred, width=COUNT_CHUNK):
        tile = 2 * SUBLANES
        lanes_acc = 4
        trips = (n_chunks * KEY_CHUNK + width - 1) // width

        def body(c, accs):
            ks = pl.multiple_of(c * width, width)
            mb = jnp.where(pred(ks, width), jnp.ones((), bf16), jnp.zeros((), bf16))
            m3 = mb.reshape(width // tile, tile, LANES)
            accs = list(accs)
            for i in range(width // tile):
                accs[i % lanes_acc] = accs[i % lanes_acc] + m3[i]
            return tuple(accs)

        zero = jnp.zeros((tile, LANES), bf16)
        accs = lax.fori_loop(0, trips, body, (zero,) * lanes_acc)
        tot = (accs[0].astype(f32) + accs[1].astype(f32)) + (accs[2].astype(f32) + accs[3].astype(f32))
        return jnp.sum(tot, axis=0, keepdims=True)

    def search16(ref, target):
        def cnt_ge(cand):
            c16 = cand.astype(i16)
            return count(lambda ks, w: ref[pl.ds(ks, w), :] >= c16)

        zero = jnp.zeros((1, LANES), jnp.int32)
        c0 = cnt_ge(zero)
        ok0 = c0 >= target
        t0 = jnp.where(ok0, zero, jnp.full((1, LANES), MIN16, jnp.int32))
        ct0 = jnp.where(ok0, c0, jnp.full((1, LANES), 1e9, f32))

        def body(i, st):
            t, ct = st
            cand = t | lax.shift_left(jnp.int32(1), 14 - i)
            c = cnt_ge(cand)
            ok = c >= target
            return jnp.where(ok, cand, t), jnp.where(ok, c, ct)

        return lax.fori_loop(0, 15, body, (t0, ct0))

    t_hi, _ = search16(hi_ref, jnp.full((1, LANES), kf, f32))
    thi16 = t_hi.astype(i16)
    need2 = kf - count(lambda ks, w: hi_ref[pl.ds(ks, w), :] > thi16)

    def mk_lom(c, carry):
        ks = pl.multiple_of(c * KEY_CHUNK, KEY_CHUNK)
        eq = hi_ref[pl.ds(ks, KEY_CHUNK), :] == thi16
        lom_ref[pl.ds(ks, KEY_CHUNK), :] = jnp.where(eq, lo_ref[pl.ds(ks, KEY_CHUNK), :],
                                                     jnp.full((), MIN16, i16))
        return carry

    lax.fori_loop(0, n_chunks, mk_lom, 0)
    t_lo, ct = search16(lom_ref, need2)
    tlo16 = t_lo.astype(i16)
    settled = jnp.logical_or(ct == need2, t_hi == MIN16)
    has_tie = jnp.max(jnp.where(settled, 0.0, 1.0)) > 0.0

    def tie_search():
        need3 = need2 - count(lambda ks, w: lom_ref[pl.ds(ks, w), :] > tlo16)
        big = jnp.full((), 32767, i16)

        def mk_pos(c, carry):
            ks = pl.multiple_of(c * KEY_CHUNK, KEY_CHUNK)
            pos = (ks + krow_full).astype(i16)
            tie_pos = jnp.where(lo_ref[pl.ds(ks, KEY_CHUNK), :] == tlo16, pos, big)
            lom_ref[pl.ds(ks, KEY_CHUNK), :] = jnp.where(hi_ref[pl.ds(ks, KEY_CHUNK), :] == thi16, tie_pos, big)
            return carry

        lax.fori_loop(0, n_chunks, mk_pos, 0)

        def jbit(i, j):
            cand = j | lax.shift_left(jnp.int32(1), 12 - i)
            c16 = cand.astype(i16)
            c = count(lambda ks, w: lom_ref[pl.ds(ks, w), :] < c16, KEY_CHUNK)
            return jnp.where(c <= need3, cand, j)

        return lax.fori_loop(0, 13, jbit, jnp.zeros((1, LANES), jnp.int32))

    jsel = lax.cond(has_tie, tie_search, lambda: jnp.full((1, LANES), 2 * SEQ, jnp.int32))
    jsel16 = jsel.astype(i16)

    low = lax.broadcasted_iota(jnp.int32, (1, LANES), 1) < HEAD_DIM
    pairs = N_HEADS // 2
    m_ref[...] = jnp.full(m_ref.shape, NEG_BIG, f32)
    acc_ref[...] = jnp.zeros_like(acc_ref)
    q = q_ref[...]
    qpair = []
    for j in range(pairs):
        blk = q[:, j * LANES:(j + 1) * LANES]
        nil = jnp.zeros_like(blk)
        qpair.append(jnp.concatenate([jnp.where(low, blk, nil), jnp.where(low, nil, blk)], axis=0))
    open16 = jnp.zeros((), bf16)
    shut16 = jnp.full((), NEG_BIG, bf16)

    def qk_all(c, lg_ref):
        ks = pl.multiple_of(c * KEY_CHUNK, KEY_CHUNK)
        for j in range(pairs):
            lg_ref[j] = _dot_nt(qpair[j], k_ref[pl.ds(ks, KEY_CHUNK), j * LANES:(j + 1) * LANES])

    def soft_pv(c, lg_ref):
        ks = pl.multiple_of(c * KEY_CHUNK, KEY_CHUNK)
        hk = hi_ref[pl.ds(ks, KEY_CHUNK), :]
        lk = lo_ref[pl.ds(ks, KEY_CHUNK), :]
        pos = (ks + krow_full).astype(i16)
        b = jnp.where(lk == tlo16, jnp.where(pos < jsel16, open16, shut16), shut16)
        b = jnp.where(hk == thi16, jnp.where(lk > tlo16, open16, b), shut16)
        b = jnp.where(hk > thi16, open16, b)
        b = jnp.where(hk == MIN16, shut16, b)
        bias = b.astype(f32).T
        bias2 = jnp.concatenate([bias, bias], axis=0)
        for j in range(pairs):
            lg = lg_ref[j] + bias2
            m_old = m_ref[j]
            m_new = jnp.maximum(m_old, jnp.max(lg, axis=1, keepdims=True))
            p = jnp.exp2(lg - m_new).astype(bf16)
            a = jnp.exp2(m_old - m_new)
            pv = _dot(p, va_ref[pl.ds(ks, KEY_CHUNK), 2 * j * LANES:(2 * j + 2) * LANES])
            own = jnp.concatenate([pv[0:Q_BLOCK, 0:LANES], pv[Q_BLOCK:2 * Q_BLOCK, LANES:2 * LANES]], axis=0)
            acc_ref[j] = a * acc_ref[j] + own
            m_ref[j] = m_new

    qk_all(0, lga_ref)
    last = n_chunks - 1

    def att_body(i, carry):
        qk_all(2 * i + 1, lgb_ref)
        soft_pv(2 * i, lga_ref)
        qk_all(jnp.minimum(2 * i + 2, last), lga_ref)
        soft_pv(2 * i + 1, lgb_ref)
        return carry

    lax.fori_loop(0, n_chunks // 2, att_body, 0)

    @pl.when(lax.rem(n_chunks, 2) == 1)
    def _():
        soft_pv(last, lga_ref)

    for j in range(pairs):
        acc = acc_ref[j]
        out = acc / pltpu.roll(acc, HEAD_DIM, 1)
        o_ref[:, j * LANES:(j + 1) * LANES] = jnp.where(low, out[0:Q_BLOCK], out[Q_BLOCK:2 * Q_BLOCK]).astype(bf16)


def _dsa_prompt(qb, qit, wit, kb, va, kib):
    nq = SEQ // Q_BLOCK
    qrow = lambda c: pl.BlockSpec((None, Q_BLOCK, c), lambda b, t: (b, t, 0))
    qcol = lambda r: pl.BlockSpec((None, r, Q_BLOCK), lambda b, t: (b, 0, t))
    seq = lambda c: pl.BlockSpec((None, SEQ, c), lambda b, t: (b, 0, 0))
    return pl.pallas_call(
        _dsa_prompt_kernel,
        grid=(BATCH, nq),
        in_specs=[qrow(ATT_W), qcol(IDX_HEADS * IDX_DIM), qcol(IDX_HEADS), seq(ATT_W), seq(2 * ATT_W),
                  seq(IDX_DIM)],
        out_specs=qrow(ATT_W),
        out_shape=jax.ShapeDtypeStruct((BATCH, SEQ, ATT_W), bf16),
        scratch_shapes=[pltpu.VMEM((SEQ, Q_BLOCK), jnp.int16),
                        pltpu.VMEM((SEQ, Q_BLOCK), jnp.int16),
                        pltpu.VMEM((SEQ, Q_BLOCK), jnp.int16),
                        pltpu.VMEM((N_HEADS // 2, 2 * Q_BLOCK, 1), f32),
                        pltpu.VMEM((N_HEADS // 2, 2 * Q_BLOCK, LANES), f32),
                        pltpu.VMEM((N_HEADS // 2, 2 * Q_BLOCK, KEY_CHUNK), f32),
                        pltpu.VMEM((N_HEADS // 2, 2 * Q_BLOCK, KEY_CHUNK), f32)],
        compiler_params=pltpu.CompilerParams(dimension_semantics=("arbitrary", "arbitrary"),
                                             vmem_limit_bytes=VMEM_LIMIT),
        name="dsa_prompt",
    )(qb, qit, wit, kb, va, kib)


S_KEYS = PAST_LEN + PAGE_SIZE
BITS_PER_PASS = 4
SEQ_PER_STEP = 2


def _dsa_sample_kernel(pt_ref, q_ref, qi_ref, wi_ref, kin_ref, kn_ref, vn_ref, *rest):
    g_n = SEQ_PER_STEP
    kidx_refs = rest[0:g_n * N_PAGES]
    k_refs = rest[g_n * N_PAGES:2 * g_n * N_PAGES]
    v_refs = rest[2 * g_n * N_PAGES:3 * g_n * N_PAGES]
    o_ref = rest[3 * g_n * N_PAGES]
    del pt_ref
    rows = DEC_SEQ * IDX_HEADS
    kf = float(TOPK)
    kpos = lax.broadcasted_iota(jnp.int32, (1, S_KEYS), 1)
    tq = lax.broadcasted_iota(jnp.int32, (DEC_SEQ, 1), 0)
    zero = jnp.zeros((DEC_SEQ, 1), jnp.int32)

    def pad_keys(x):
        return jnp.concatenate([x, jnp.zeros((PAGE_SIZE - x.shape[0], x.shape[1]), x.dtype)], axis=0)

    def count(pred):
        return jnp.sum(jnp.where(pred, 1.0, 0.0), axis=1, keepdims=True)

    def threshold(g):
        qi = qi_ref[g].astype(bf16)
        wi = wi_ref[g]

        def idx_score(d):
            r = jnp.maximum(d, 0.0) * wi
            return jnp.sum(r.reshape(DEC_SEQ, IDX_HEADS, PAGE_SIZE), axis=1)

        parts = [idx_score(_dot(qi, kidx_refs[g * N_PAGES + j][...].astype(bf16))) for j in range(N_PAGES)]
        parts.append(idx_score(_dot_nt(qi, pad_keys(kin_ref[g]).astype(bf16))))
        s = jnp.concatenate(parts, axis=1)
        keys = jnp.where((kpos - PAST_LEN) <= tq, _order_key(s), INT_MIN)
        c0 = count(keys >= zero)
        thr = jnp.where(c0 >= kf, zero, jnp.full((DEC_SEQ, 1), INT_MIN, jnp.int32))
        cthr = jnp.where(c0 >= kf, c0, 1e9)
        shift = 31
        while shift > 0:
            nb = min(BITS_PER_PASS, shift)
            shift -= nb
            num = zero
            for c in range(1, 2 ** nb):
                cnt = count(keys >= (thr | (c << shift)))
                ok = cnt >= kf
                num = num + ok.astype(jnp.int32)
                cthr = jnp.where(ok, cnt, cthr)
            thr = thr | lax.shift_left(num, shift)
        need = kf - count(keys > thr)
        settled = jnp.logical_or(cthr == kf, thr == INT_MIN)
        return keys, thr, need, jnp.max(jnp.where(settled, 0.0, 1.0))

    sel_in = [threshold(g) for g in range(g_n)]
    unsettled = sel_in[0][3]
    for g in range(1, g_n):
        unsettled = jnp.maximum(unsettled, sel_in[g][3])

    def tie_search():
        out = []
        for keys, thr, need, _ in sel_in:
            def jbit(i, j, keys=keys, thr=thr, need=need):
                cand = j | lax.shift_left(jnp.int32(1), 12 - i)
                c = count(jnp.logical_and(keys == thr, kpos < cand))
                return jnp.where(c <= need, cand, j)
            out.append(lax.fori_loop(0, 13, jbit, zero))
        return tuple(out)

    no_tie = tuple(jnp.full((DEC_SEQ, 1), 2 * S_KEYS, jnp.int32) for _ in range(g_n))
    jsels = lax.cond(unsettled > 0.0, tie_search, lambda: no_tie)

    hrow = lax.broadcasted_iota(jnp.int32, (rows, ATT_W), 0) % N_HEADS
    hcol = lax.broadcasted_iota(jnp.int32, (rows, ATT_W), 1) // HEAD_DIM
    own = hrow == hcol
    for g in range(g_n):
        keys, thr, _, _ = sel_in[g]
        sel = jnp.logical_and(
            jnp.logical_or(keys > thr, jnp.logical_and(keys == thr, kpos < jsels[g])), keys != INT_MIN)
        bias4 = jnp.where(sel, 0.0, NEG_BIG)
        bias = jnp.broadcast_to(bias4[:, None, :], (DEC_SEQ, N_HEADS, S_KEYS)).reshape(rows, S_KEYS)

        q4 = q_ref[g] * ATT_SCALE
        q32 = jnp.broadcast_to(q4[:, None, :], (DEC_SEQ, N_HEADS, ATT_W)).reshape(rows, ATT_W)
        qbd = jnp.where(own, q32, 0.0).astype(bf16)

        kn = pad_keys(kn_ref[g]).astype(bf16)
        vn = pad_keys(vn_ref[g]).astype(bf16)
        lg = jnp.concatenate([_dot(qbd, k_refs[g * N_PAGES + j][...].astype(bf16)) for j in range(N_PAGES)]
                             + [_dot_nt(qbd, kn)], axis=1) + bias
        m = jnp.max(lg, axis=1, keepdims=True)
        p = jnp.exp(lg - m)
        l = jnp.sum(p, axis=1, keepdims=True)
        pb = p.astype(bf16)
        pv = _dot(pb[:, PAST_LEN:], vn)
        for j in range(N_PAGES):
            pv = pv + _dot_nt(pb[:, j * PAGE_SIZE:(j + 1) * PAGE_SIZE], v_refs[g * N_PAGES + j][...].astype(bf16))
        out = jnp.where(own, pv, 0.0) / l
        o_ref[g] = jnp.sum(out.reshape(DEC_SEQ, N_HEADS, ATT_W), axis=1)


def _dsa_sample(page_table, q, qi, wi, ki_new, k_new, v_new, kidx_pages, k_pages, v_pages, layer):
    n_pool = kidx_pages.shape[0] // (DEPTH // 2)
    rows = DEC_SEQ * IDX_HEADS
    g_n = SEQ_PER_STEP
    per_seq = lambda r, c: pl.BlockSpec((g_n, r, c), lambda s, pt: (s, 0, 0))

    def page(r, g, j):
        return pl.BlockSpec((None, r, PAGE_SIZE), lambda s, pt: (layer * n_pool + pt[s * g_n + g, j], 0, 0))

    pages = lambda r: [page(r, g, j) for g in range(g_n) for j in range(N_PAGES)]
    in_specs = [per_seq(DEC_SEQ, ATT_W), per_seq(rows, IDX_DIM), per_seq(rows, 1),
                per_seq(SUBLANES, IDX_DIM), per_seq(SUBLANES, ATT_W), per_seq(SUBLANES, ATT_W)]
    in_specs += pages(IDX_DIM) + pages(ATT_W) + pages(ATT_W)
    grid_spec = pltpu.PrefetchScalarGridSpec(
        num_scalar_prefetch=1, grid=(DEC_BATCH // g_n,), in_specs=in_specs,
        out_specs=per_seq(DEC_SEQ, ATT_W))
    n_in = g_n * N_PAGES
    return pl.pallas_call(
        _dsa_sample_kernel,
        grid_spec=grid_spec,
        out_shape=jax.ShapeDtypeStruct((DEC_BATCH, DEC_SEQ, ATT_W), f32),
        compiler_params=pltpu.CompilerParams(dimension_semantics=("arbitrary",),
                                             vmem_limit_bytes=VMEM_LIMIT),
        name="dsa_sample",
    )(page_table, q, qi, wi, ki_new, k_new, v_new,
      *([kidx_pages] * n_in), *([k_pages] * n_in), *([v_pages] * n_in))


def _rope_tables(pos):
    half = ROT // 2
    inv = jnp.power(jnp.float32(ROPE_THETA), -jnp.arange(half, dtype=f32) * 2.0 / ROT)
    ang = pos.astype(f32)[:, None] * inv[None, :]
    cos, sin = jnp.cos(ang), jnp.sin(ang)
    t = pos.shape[0]
    rest = HEAD_DIM - ROT
    c64 = jnp.concatenate([cos, cos, jnp.ones((t, rest), f32)], axis=1)
    s1 = jnp.concatenate([jnp.zeros((t, half), f32), sin, jnp.zeros((t, rest), f32)], axis=1)
    s2 = jnp.concatenate([-sin, jnp.zeros((t, half + rest), f32)], axis=1)
    two = lambda a: jnp.concatenate([a, a], axis=1)
    return two(c64), two(s1), two(s2)


def _to_tm(a):
    return a.transpose(1, 0, 2).reshape(a.shape[1] * a.shape[0], a.shape[2])


def _from_tm(a, t=DEC_SEQ):
    return a.reshape(t, DEC_BATCH, a.shape[-1]).transpose(1, 0, 2)


def kernel(x_prompt, x_sample, state_b_conv, state_c_pool, cache_k, cache_v, cache_kidx, state_ffn_conv, page_table, w_in_even, a_ln_g, a_ln_b, a_ws, a_bs, b_conv_w, w_out_even, w_in_odd, c_w, c_scale, w_out_odd, ln_mix_g, ln_mix_b, ffn_w_up, ffn_conv_w, ffn_conv_b, ffn_w_down, ln_ffn_g, ln_ffn_b):
    row = lambda a: a.reshape(1, -1)
    xp = x_prompt
    xs = _to_tm(x_sample)
    outs = {}
    ffn_p, ffn_s = [], []

    def post(layer, xp, xs, cap, cbp, cas, cbs, wo):
        wts = (wo.astype(bf16), row(ln_mix_g[layer]), row(ln_mix_b[layer]), ffn_w_up[layer].astype(bf16),
               ffn_conv_w[layer], row(ffn_conv_b[layer]), ffn_w_down[layer].astype(bf16),
               row(ln_ffn_g[layer]), row(ln_ffn_b[layer]))
        xp, st_p = _post_prompt(xp, cap, cbp, *wts)
        xs, st_s = _post_sample(xs, cas, cbs, *wts, _to_tm(state_ffn_conv[layer]))
        ffn_p.append(st_p[:, SUBLANES - (CONV_W - 1):])
        ffn_s.append(_from_tm(st_s, CONV_W - 1))
        return xp, xs

    for layer in range(DEPTH):
        if layer % 2 == 0:
            e = layer // 2
            w = w_in_even[e].astype(bf16)
            lg, lb = row(a_ln_g[e]), row(a_ln_b[e])
            bias = jnp.repeat(a_bs[e].T, CHUNK, axis=1)
            wcoef = jnp.repeat(a_ws[e][:, :DEC_SEQ, :DEC_SEQ].transpose(1, 2, 0).reshape(DEC_SEQ * DEC_SEQ, A_GROUPS),
                               CHUNK, axis=1)
            a_p, b_p, bst_p = _even_pre_prompt(xp, w, lg, lb, a_ws[e], bias, b_conv_w[e])
            a_s, b_s, v_s, e_s = _even_pre_sample(xs, w, lg, lb, wcoef, bias, b_conv_w[e],
                                                  _to_tm(state_b_conv[e]))
            outs.setdefault("a_v_s", []).append(_from_tm(v_s))
            outs.setdefault("b_p", []).append(bst_p[:, SUBLANES - (CONV_W - 1):])
            outs.setdefault("b_s", []).append(_from_tm(e_s)[:, DEC_SEQ - (CONV_W - 1):])
            xp, xs = post(layer, xp, xs, a_p, b_p, a_s, b_s, w_out_even[e])
        else:
            o = layer // 2
            wm = w_in_odd[o][:, :O_KI].astype(bf16)
            wki = w_in_odd[o][:, O_KI:O_WI]
            wwi = w_in_odd[o][:, O_WI:]
            wt = jnp.concatenate([wki, wki, wwi, jnp.zeros((D_MODEL, LANES - IDX_HEADS), f32)],
                                 axis=1).astype(bf16)
            cwt = c_w[o].astype(bf16)
            cs = row(c_scale[o])
            tabs_p = _rope_tables(jnp.arange(SEQ, dtype=jnp.int32))
            pos_s = PAST_LEN + jnp.repeat(jnp.arange(DEC_SEQ, dtype=jnp.int32), DEC_BATCH)
            tabs_s = _rope_tables(pos_s)
            (k_p, v_p, ki_p, qb, kb, va, qit, kib, wit, co_p, ph_p) = _odd_pre_prompt(
                xp, wm, wt, *tabs_p, cwt, cs)
            att_p = _dsa_prompt(qb, qit, wit, kb, va, kib)
            (k_s, v_s2, ki_s, q_s, qi_s, wi_s, co_s, p_s) = _odd_pre_sample(
                xs, wm, wt, *tabs_s, cwt, cs, _to_tm(state_c_pool[o]))
            rows = DEC_SEQ * IDX_HEADS
            pad8 = lambda a: jnp.pad(_from_tm(a), ((0, 0), (0, SUBLANES - DEC_SEQ), (0, 0)))
            page_t = lambda c, w: jnp.moveaxis(c, 2, -1).reshape(-1, w, PAGE_SIZE)
            att_s = _dsa_sample(
                page_table,
                _from_tm(q_s),
                _from_tm(qi_s).reshape(DEC_BATCH, rows, IDX_DIM),
                _from_tm(wi_s)[:, :, :IDX_HEADS].reshape(DEC_BATCH, rows, 1),
                pad8(ki_s), pad8(k_s), pad8(v_s2),
                page_t(cache_kidx, IDX_DIM), page_t(cache_k, ATT_W), page_t(cache_v, ATT_W), o)
            heads = lambda a: a.reshape(a.shape[0], a.shape[1], N_HEADS, HEAD_DIM)
            outs.setdefault("c_p", []).append(ph_p[:, 2 * SUBLANES - POOL_HIST:])
            outs.setdefault("c_s", []).append(
                jnp.concatenate([state_c_pool[o][:, DEC_SEQ:], _from_tm(p_s)], axis=1))
            outs.setdefault("k_p", []).append(heads(k_p))
            outs.setdefault("v_p", []).append(heads(v_p))
            outs.setdefault("ki_p", []).append(ki_p)
            outs.setdefault("k_s", []).append(heads(_from_tm(k_s)))
            outs.setdefault("v_s", []).append(heads(_from_tm(v_s2)))
            outs.setdefault("ki_s", []).append(_from_tm(ki_s))
            xp, xs = post(layer, xp, xs, co_p, att_p, co_s, _to_tm(att_s).astype(bf16), w_out_odd[o])

    st = lambda name: jnp.stack(outs[name])
    return (xp, _from_tm(xs), st("a_v_s"), st("b_p"), st("b_s"), st("c_p"), st("c_s"),
            st("k_p"), st("v_p"), st("ki_p"), st("k_s"), st("v_s"), st("ki_s"),
            jnp.stack(ffn_p), jnp.stack(ffn_s))
```

```python
import functools

import jax
import jax.numpy as jnp
from jax import lax
from jax.experimental import pallas as pl
from jax.experimental.pallas import tpu as pltpu

D_MODEL = 1024
BATCH = 4
SEQ = 4096
DEPTH = 2
DEC_BATCH = 128
DEC_SEQ = 4
PAST_LEN = 2048
PAGE_SIZE = 128
N_PAGES = PAST_LEN // PAGE_SIZE

A_W = D_MODEL // 2
B_W = D_MODEL // 2
C_W = D_MODEL // 2
ATT_W = D_MODEL // 2
CHUNK = 128
A_GROUPS = A_W // CHUNK
CONV_W = 3
POOL_WINDOWS = (2, 4, 8, 16)
C_GROUPS = len(POOL_WINDOWS)
C_GROUP_W = C_W // C_GROUPS
POOL_HIST = max(POOL_WINDOWS) - 1
HEAD_DIM = 64
N_HEADS = ATT_W // HEAD_DIM
IDX_HEADS = 8
IDX_DIM = 64
TOPK = 256
Q_BLOCK = 128
ROPE_THETA = 500000.0
ROT = HEAD_DIM // 4
D_FF = ((8 * D_MODEL // 3 + 127) // 128) * 128
ALPHA = (2.0 * DEPTH) ** 0.25
LN_EPS = 1e-5
ATT_SCALE = HEAD_DIM ** -0.5
LOG2E = 1.4426950408889634

E_U, E_V, E_B, E_C, E_H = 0, A_W, 2 * A_W, 2 * A_W + B_W, 2 * A_W + 2 * B_W
O_P, O_Q, O_K, O_V, O_QI = 0, C_W, C_W + ATT_W, C_W + 2 * ATT_W, C_W + 3 * ATT_W
O_KI = O_QI + IDX_HEADS * IDX_DIM
O_WI = O_KI + IDX_DIM

LANES = 128
SUBLANES = 8
VMEM_LIMIT = 56 * 1024 * 1024

TM = 512
TM_POST = 256
FF_CHUNK = 256
KEY_CHUNK = 512
COUNT_CHUNK = 1024
S_ROWS = DEC_SEQ * DEC_BATCH
INT_MIN = -2 ** 31
MIN16 = -2 ** 15
NEG_BIG = -1e30

bf16 = jnp.bfloat16
f32 = jnp.float32


def _ln(x, g, b):
    mu = jnp.mean(x, axis=-1, keepdims=True)
    xc = x - mu
    var = jnp.mean(xc * xc, axis=-1, keepdims=True)
    return xc * lax.rsqrt(var + LN_EPS) * g + b


def _gelu(x):
    c = (2.0 / jnp.pi) ** 0.5
    return x * (0.5 * (1.0 + jnp.tanh(c * (x + 0.044715 * (x * x * x)))))


def _dot(a, b):
    return jnp.dot(a, b, preferred_element_type=f32)


def _dot_nt(a, b):
    return lax.dot_general(a, b, (((1,), (1,)), ((), ())), preferred_element_type=f32)


def _prev_rows_prompt(cur, halo):
    ext = jnp.concatenate([halo, cur], axis=0)
    p1 = pltpu.roll(ext, 1, 0)[SUBLANES:]
    p2 = pltpu.roll(ext, 2, 0)[SUBLANES:]
    return p1, p2


def _prev_rows_sample(cur, hist):
    n = DEC_BATCH
    p1 = jnp.concatenate([hist[n:2 * n], cur[0:3 * n]], axis=0)
    p2 = jnp.concatenate([hist, cur[0:2 * n]], axis=0)
    return p1, p2


def _causal_conv(cur, p1, p2, w_ref, c0, width):
    return (cur * w_ref[2:3, c0:c0 + width] + p1 * w_ref[1:2, c0:c0 + width]
            + p2 * w_ref[0:1, c0:c0 + width])


def _even_pre_kernel(*refs, sample):
    if sample:
        (x_ref, w_ref, lg_ref, lb_ref, wcoef_ref, bias_ref, cw_ref, hist_ref,
         a_ref, b_ref, v_ref, e_ref) = refs
    else:
        (x_ref, w_ref, lg_ref, lb_ref, ws_ref, bias_ref, cw_ref,
         a_ref, b_ref, st_ref, halo_ref) = refs

        @pl.when(pl.program_id(1) == 0)
        def _():
            halo_ref[...] = jnp.zeros_like(halo_ref)

    xb = x_ref[...].astype(bf16)
    rows = xb.shape[0]

    def proj(c0):
        return _dot(xb, w_ref[:, c0:c0 + A_W])

    u = _gelu(proj(E_U))
    v = _ln(_gelu(proj(E_V)), lg_ref[...], lb_ref[...])
    if sample:
        v_ref[...] = v
        n = DEC_BATCH
        for t in range(DEC_SEQ):
            mix = bias_ref[t:t + 1, :]
            for s in range(t + 1):
                r = t * DEC_SEQ + s
                mix = mix + wcoef_ref[r:r + 1, :] * v[s * n:(s + 1) * n]
            a_ref[t * n:(t + 1) * n, :] = (u[t * n:(t + 1) * n] * mix).astype(bf16)
    else:
        vb = v.astype(bf16)
        ri = lax.broadcasted_iota(jnp.int32, (CHUNK, CHUNK), 0)
        ci = lax.broadcasted_iota(jnp.int32, (CHUNK, CHUNK), 1)
        for g in range(A_GROUPS):
            gs = slice(g * CHUNK, (g + 1) * CHUNK)
            wc = jnp.where(ri >= ci, ws_ref[g], 0.0).astype(bf16)
            for c in range(rows // CHUNK):
                rs = slice(c * CHUNK, (c + 1) * CHUNK)
                mix = _dot(wc, vb[rs, gs]) + bias_ref[:, gs]
                a_ref[rs, gs] = (u[rs, gs] * mix).astype(bf16)

    e = proj(E_C) * proj(E_H)
    if sample:
        e_ref[...] = e
        p1, p2 = _prev_rows_sample(e, hist_ref[...])
    else:
        p1, p2 = _prev_rows_prompt(e, halo_ref[...])
        halo_ref[...] = e[rows - SUBLANES:]
        st_ref[...] = e[rows - SUBLANES:]
    conv = _causal_conv(e, p1, p2, cw_ref, 0, B_W)
    b_ref[...] = (proj(E_B) * conv).astype(bf16)


def _full(shape):
    nd = len(shape)
    return pl.BlockSpec(shape, lambda *_: (0,) * nd)


def _even_pre_prompt(x, w, lg, lb, ws, bias, cw):
    nt = SEQ // TM
    row = lambda c: pl.BlockSpec((None, TM, c), lambda b, t: (b, t, 0))
    return pl.pallas_call(
        functools.partial(_even_pre_kernel, sample=False),
        grid=(BATCH, nt),
        in_specs=[row(D_MODEL), _full(w.shape), _full(lg.shape), _full(lb.shape), _full(ws.shape),
                  _full(bias.shape), _full(cw.shape)],
        out_specs=[row(A_W), row(B_W), pl.BlockSpec((None, SUBLANES, B_W), lambda b, t: (b, 0, 0))],
        out_shape=[jax.ShapeDtypeStruct((BATCH, SEQ, A_W), bf16),
                   jax.ShapeDtypeStruct((BATCH, SEQ, B_W), bf16),
                   jax.ShapeDtypeStruct((BATCH, SUBLANES, B_W), f32)],
        scratch_shapes=[pltpu.VMEM((SUBLANES, B_W), f32)],
        compiler_params=pltpu.CompilerParams(dimension_semantics=("arbitrary", "arbitrary"),
                                             vmem_limit_bytes=VMEM_LIMIT),
        name="even_pre_prompt",
    )(x, w, lg, lb, ws, bias, cw)


def _even_pre_sample(x, w, lg, lb, wcoef, bias, cw, hist):
    args = (x, w, lg, lb, wcoef, bias, cw, hist)
    return pl.pallas_call(
        functools.partial(_even_pre_kernel, sample=True),
        grid=(1,),
        in_specs=[_full(a.shape) for a in args],
        out_specs=[_full((S_ROWS, A_W)), _full((S_ROWS, B_W)), _full((S_ROWS, A_W)), _full((S_ROWS, B_W))],
        out_shape=[jax.ShapeDtypeStruct((S_ROWS, A_W), bf16),
                   jax.ShapeDtypeStruct((S_ROWS, B_W), bf16),
                   jax.ShapeDtypeStruct((S_ROWS, A_W), f32),
                   jax.ShapeDtypeStruct((S_ROWS, B_W), f32)],
        compiler_params=pltpu.CompilerParams(dimension_semantics=("arbitrary",),
                                             vmem_limit_bytes=VMEM_LIMIT),
        name="even_pre_sample",
    )(*args)


def _post_kernel(*refs, sample):
    if sample:
        (x_ref, ca_ref, cb_ref, wo_ref, g1_ref, b1_ref, wu_ref, cw_ref, cb2_ref, wd_ref, g2_ref, b2_ref,
         hist_ref, o_ref, st_ref, acc_ref) = refs
    else:
        (x_ref, ca_ref, cb_ref, wo_ref, g1_ref, b1_ref, wu_ref, cw_ref, cb2_ref, wd_ref, g2_ref, b2_ref,
         o_ref, st_ref, acc_ref, halo_ref) = refs

        @pl.when(pl.program_id(1) == 0)
        def _():
            halo_ref[...] = jnp.zeros_like(halo_ref)

    half = ca_ref.shape[-1]
    y = _dot(ca_ref[...], wo_ref[0:half, :]) + _dot(cb_ref[...], wo_ref[half:2 * half, :])
    x1 = _ln(ALPHA * x_ref[...] + y, g1_ref[...], b1_ref[...])
    x1b = x1.astype(bf16)
    rows = x1b.shape[0]

    def up(c0):
        return (_dot(x1b, wu_ref[:, c0:c0 + FF_CHUNK]), _dot(x1b, wu_ref[:, D_FF + c0:D_FF + c0 + FF_CHUNK]))

    def conv_half(hu, c0):
        if sample:
            p1, p2 = _prev_rows_sample(hu, hist_ref[:, c0:c0 + FF_CHUNK])
            st_ref[:, c0:c0 + FF_CHUNK] = hu[2 * DEC_BATCH:]
        else:
            p1, p2 = _prev_rows_prompt(hu, halo_ref[:, c0:c0 + FF_CHUNK])
            halo_ref[:, c0:c0 + FF_CHUNK] = hu[rows - SUBLANES:]
            st_ref[:, c0:c0 + FF_CHUNK] = hu[rows - SUBLANES:]
        return _causal_conv(hu, p1, p2, cw_ref, c0, FF_CHUNK) + cb2_ref[:, c0:c0 + FF_CHUNK]

    n_ff = D_FF // FF_CHUNK
    nxt = up(0)
    for c in range(n_ff):
        c0 = c * FF_CHUNK
        cur = nxt
        if c + 1 < n_ff:
            nxt = up(c0 + FF_CHUNK)
        hg = conv_half(cur[0], c0)
        hv = conv_half(cur[1], D_FF + c0)
        act = (hg * jax.nn.sigmoid(hg) * hv).astype(bf16)
        part = _dot(act, wd_ref[c0:c0 + FF_CHUNK, :])
        if c == 0:
            acc_ref[...] = part
        else:
            acc_ref[...] += part
    o_ref[...] = _ln(ALPHA * x1 + acc_ref[...], g2_ref[...], b2_ref[...])


def _layer_spec(a, layer, **kw):
    return pl.BlockSpec((None,) + a.shape[1:], lambda *_: (layer,) + (0,) * (a.ndim - 1), **kw)


def _post_prompt(layer, x, ca, cb, wo, g1, b1, wu, cw, cb2, wd, g2, b2):
    nt = SEQ // TM_POST
    row = lambda c: pl.BlockSpec((None, TM_POST, c), lambda b, t: (b, t, 0))
    const = lambda a: pl.BlockSpec(a.shape, lambda b, t: (0,) * a.ndim, pipeline_mode=pl.Buffered(1))
    stacked = lambda a: _layer_spec(a, layer, pipeline_mode=pl.Buffered(1))
    return pl.pallas_call(
        functools.partial(_post_kernel, sample=False),
        grid=(BATCH, nt),
        in_specs=[row(D_MODEL), row(ca.shape[-1]), row(cb.shape[-1]), const(wo)]
                 + [stacked(a) for a in (g1, b1, wu, cw, cb2, wd, g2, b2)],
        out_specs=[row(D_MODEL), pl.BlockSpec((None, SUBLANES, 2 * D_FF), lambda b, t: (b, 0, 0))],
        out_shape=[jax.ShapeDtypeStruct((BATCH, SEQ, D_MODEL), f32),
                   jax.ShapeDtypeStruct((BATCH, SUBLANES, 2 * D_FF), f32)],
        scratch_shapes=[pltpu.VMEM((TM_POST, D_MODEL), f32), pltpu.VMEM((SUBLANES, 2 * D_FF), f32)],
        compiler_params=pltpu.CompilerParams(dimension_semantics=("arbitrary", "arbitrary"),
                                             vmem_limit_bytes=VMEM_LIMIT),
        name="post_prompt",
    )(x, ca, cb, wo, g1, b1, wu, cw, cb2, wd, g2, b2)


def _post_sample(layer, x, ca, cb, wo, g1, b1, wu, cw, cb2, wd, g2, b2, hist):
    args = (x, ca, cb, wo, g1, b1, wu, cw, cb2, wd, g2, b2, hist)
    full = lambda a: _full(a.shape)
    stacked = lambda a: _layer_spec(a, layer)
    return pl.pallas_call(
        functools.partial(_post_kernel, sample=True),
        grid=(1,),
        in_specs=[full(x), full(ca), full(cb), full(wo)]
                 + [stacked(a) for a in (g1, b1, wu, cw, cb2, wd, g2, b2)] + [full(hist)],
        out_specs=[_full((S_ROWS, D_MODEL)), _full((2 * DEC_BATCH, 2 * D_FF))],
        out_shape=[jax.ShapeDtypeStruct((S_ROWS, D_MODEL), f32),
                   jax.ShapeDtypeStruct((2 * DEC_BATCH, 2 * D_FF), f32)],
        scratch_shapes=[pltpu.VMEM((S_ROWS, D_MODEL), f32)],
        compiler_params=pltpu.CompilerParams(dimension_semantics=("arbitrary",),
                                             vmem_limit_bytes=VMEM_LIMIT),
        name="post_sample",
    )(*args)


def _rope(blk, cos, s1, s2):
    return blk * cos + pltpu.roll(blk, ROT // 2, 1) * s1 + pltpu.roll(blk, LANES - ROT // 2, 1) * s2


def _odd_pre_kernel(*refs, sample):
    if sample:
        (x_ref, wm_ref, wt_ref, cos_ref, s1_ref, s2_ref, cwt_ref, cs_ref, hist_ref,
         k_ref, v_ref, ki_ref, q_ref, qi_ref, wi_ref, co_ref, p_ref) = refs
    else:
        (x_ref, wm_ref, wt_ref, cos_ref, s1_ref, s2_ref, cwt_ref, cs_ref,
         k_ref, v_ref, ki_ref, qb_ref, kb_ref, va_ref, qit_ref, kib_ref, wit_ref, co_ref, ph_ref,
         halo_ref) = refs

        @pl.when(pl.program_id(1) == 0)
        def _():
            halo_ref[...] = jnp.zeros_like(halo_ref)

    xb = x_ref[...].astype(bf16)
    rows = xb.shape[0]
    cos, s1, s2 = cos_ref[...], s1_ref[...], s2_ref[...]

    def proj(c0):
        return _dot(xb, wm_ref[:, c0:c0 + C_W])

    def rope_wide(h):
        return jnp.concatenate(
            [_rope(h[:, j * LANES:(j + 1) * LANES], cos, s1, s2) for j in range(h.shape[1] // LANES)], axis=1)

    q = rope_wide(proj(O_Q))
    k = rope_wide(proj(O_K))
    v = proj(O_V)
    qi = rope_wide(proj(O_QI))
    tail = _dot(xb, wt_ref[...])
    ki2 = _rope(tail[:, 0:LANES], cos, s1, s2)
    k_ref[...] = k
    v_ref[...] = v
    ki_ref[...] = ki2[:, 0:IDX_DIM]
    wi = tail[:, LANES:2 * LANES]
    if sample:
        q_ref[...] = q
        qi_ref[...] = qi
        wi_ref[...] = wi
    else:
        qb_ref[...] = (q * (ATT_SCALE * LOG2E)).astype(bf16)
        kb_ref[...] = k.astype(bf16)
        low = lax.broadcasted_iota(jnp.int32, (1, LANES), 1) < HEAD_DIM
        for j in range(N_HEADS // 2):
            vj = v[:, j * LANES:(j + 1) * LANES]
            va_ref[:, 2 * j * LANES:(2 * j + 1) * LANES] = jnp.where(low, vj, 1.0).astype(bf16)
            va_ref[:, (2 * j + 1) * LANES:(2 * j + 2) * LANES] = jnp.where(low, 1.0, vj).astype(bf16)
        qit_ref[...] = qi.T.astype(bf16)
        kib_ref[...] = ki2[:, 0:IDX_DIM].astype(bf16)
        wit_ref[...] = wi.T[0:IDX_HEADS]

    p = proj(O_P)
    if sample:
        p_ref[...] = p
        n = DEC_BATCH
        hist = hist_ref[...]

        def slab(j, gs):
            if j < POOL_HIST:
                return hist[j * n:(j + 1) * n, gs]
            return p[(j - POOL_HIST) * n:(j - POOL_HIST + 1) * n, gs]

        for g, w in enumerate(POOL_WINDOWS):
            gs = slice(g * C_GROUP_W, (g + 1) * C_GROUP_W)
            for t in range(DEC_SEQ):
                win = slab(POOL_HIST + t, gs)
                for i in range(1, w):
                    win = win + slab(POOL_HIST + t - i, gs)
                pooled = win / float(w) - p[t * n:(t + 1) * n, gs]
                co = _dot(pooled.astype(bf16), cwt_ref[g]) * cs_ref[:, gs]
                co_ref[t * n:(t + 1) * n, gs] = co.astype(bf16)
    else:
        hrows = 2 * SUBLANES
        ext = jnp.concatenate([halo_ref[...], p], axis=0)
        pos = pl.program_id(1) * rows + lax.broadcasted_iota(jnp.int32, (rows, 1), 0)
        for g, w in enumerate(POOL_WINDOWS):
            gs = slice(g * C_GROUP_W, (g + 1) * C_GROUP_W)
            s = ext[:, gs]
            step = 1
            while step < w:
                s = s + pltpu.roll(s, step, 0)
                step *= 2
            cnt = jnp.minimum(w, pos + 1).astype(f32)
            pooled = s[hrows:] / cnt - p[:, gs]
            co = _dot(pooled.astype(bf16), cwt_ref[g]) * cs_ref[:, gs]
            co_ref[:, gs] = co.astype(bf16)
        halo_ref[...] = p[rows - hrows:]
        ph_ref[...] = p[rows - hrows:]


def _odd_pre_prompt(x, wm, wt, cos, s1, s2, cwt, cs):
    nt = SEQ // TM
    row = lambda c: pl.BlockSpec((None, TM, c), lambda b, t: (b, t, 0))
    col = lambda r: pl.BlockSpec((None, r, TM), lambda b, t: (b, 0, t))
    tab = pl.BlockSpec((TM, LANES), lambda b, t: (t, 0))
    sds = lambda c, dt: jax.ShapeDtypeStruct((BATCH, SEQ, c), dt)
    sdt = lambda r, dt: jax.ShapeDtypeStruct((BATCH, r, SEQ), dt)
    hrows = 2 * SUBLANES
    return pl.pallas_call(
        functools.partial(_odd_pre_kernel, sample=False),
        grid=(BATCH, nt),
        in_specs=[row(D_MODEL), _full(wm.shape), _full(wt.shape), tab, tab, tab, _full(cwt.shape),
                  _full(cs.shape)],
        out_specs=[row(ATT_W), row(ATT_W), row(IDX_DIM), row(ATT_W), row(ATT_W), row(2 * ATT_W),
                   col(IDX_HEADS * IDX_DIM), row(IDX_DIM), col(IDX_HEADS), row(C_W),
                   pl.BlockSpec((None, hrows, C_W), lambda b, t: (b, 0, 0))],
        out_shape=[sds(ATT_W, f32), sds(ATT_W, f32), sds(IDX_DIM, f32), sds(ATT_W, bf16), sds(ATT_W, bf16),
                   sds(2 * ATT_W, bf16), sdt(IDX_HEADS * IDX_DIM, bf16), sds(IDX_DIM, bf16), sdt(IDX_HEADS, f32),
                   sds(C_W, bf16), jax.ShapeDtypeStruct((BATCH, hrows, C_W), f32)],
        scratch_shapes=[pltpu.VMEM((hrows, C_W), f32)],
        compiler_params=pltpu.CompilerParams(dimension_semantics=("arbitrary", "arbitrary"),
                                             vmem_limit_bytes=VMEM_LIMIT),
        name="odd_pre_prompt",
    )(x, wm, wt, cos, s1, s2, cwt, cs)


def _odd_pre_sample(x, wm, wt, cos, s1, s2, cwt, cs, hist):
    args = (x, wm, wt, cos, s1, s2, cwt, cs, hist)
    sds = lambda c, dt: jax.ShapeDtypeStruct((S_ROWS, c), dt)
    outs = [sds(ATT_W, f32), sds(ATT_W, f32), sds(IDX_DIM, f32), sds(ATT_W, f32), sds(ATT_W, f32),
            sds(LANES, f32), sds(C_W, bf16), sds(C_W, f32)]
    return pl.pallas_call(
        functools.partial(_odd_pre_kernel, sample=True),
        grid=(1,),
        in_specs=[_full(a.shape) for a in args],
        out_specs=[_full(o.shape) for o in outs],
        out_shape=outs,
        compiler_params=pltpu.CompilerParams(dimension_semantics=("arbitrary",),
                                             vmem_limit_bytes=VMEM_LIMIT),
        name="odd_pre_sample",
    )(*args)


def _order_key(s):
    s = jnp.where(s == 0.0, 0.0, s)
    bits = lax.bitcast_convert_type(s, jnp.int32)
    return bits ^ ((bits >> 31) & 0x7FFFFFFF)


def _dsa_prompt_kernel(q_ref, qit_ref, wit_ref, k_ref, va_ref, ki_ref, o_ref,
                       hi_ref, lo_ref, lom_ref, m_ref, acc_ref, lga_ref, lgb_ref, da_ref, db_ref):
    qb = pl.program_id(1)
    n_chunks = qb // (KEY_CHUNK // Q_BLOCK) + 1
    qpos = qb * Q_BLOCK + lax.broadcasted_iota(jnp.int32, (1, Q_BLOCK), 1)
    krow = lax.broadcasted_iota(jnp.int32, (KEY_CHUNK, 1), 0)
    krow_full = lax.broadcasted_iota(jnp.int32, (KEY_CHUNK, LANES), 0)
    kf = float(TOPK)
    i16 = jnp.int16

    @pl.when(qb == 0)
    def _():
        for ref in (hi_ref, lo_ref, lom_ref):
            ref[...] = jnp.full(ref.shape, MIN16, i16)

    qit = qit_ref[...]
    stat = jnp.concatenate([qit[h * IDX_DIM:(h + 1) * IDX_DIM, :] for h in range(IDX_HEADS)], axis=1)
    wit = wit_ref[...]

    def idx_dots(c, d_ref):
        ks = pl.multiple_of(c * KEY_CHUNK, KEY_CHUNK)
        d_ref[...] = _dot(ki_ref[pl.ds(ks, KEY_CHUNK), :], stat)

    def idx_keys(c, d_ref):
        ks = pl.multiple_of(c * KEY_CHUNK, KEY_CHUNK)
        s = jnp.zeros((KEY_CHUNK, Q_BLOCK), f32)
        for h in range(IDX_HEADS):
            s = s + wit[h:h + 1, :] * jnp.maximum(d_ref[:, h * Q_BLOCK:(h + 1) * Q_BLOCK], 0.0)
        key = jnp.where((ks + krow) <= qpos, _order_key(s), INT_MIN)
        hi_ref[pl.ds(ks, KEY_CHUNK), :] = (key >> 16).astype(i16)
        lo_ref[pl.ds(ks, KEY_CHUNK), :] = ((key & 0xFFFF) + MIN16).astype(i16)

    def two_chunk_pipeline(produce, consume, buf_a, buf_b):
        last = n_chunks - 1
        produce(0, buf_a)

        def body(i, carry):
            produce(2 * i + 1, buf_b)
            consume(2 * i, buf_a)
            produce(jnp.minimum(2 * i + 2, last), buf_a)
            consume(2 * i + 1, buf_b)
            return carry

        lax.fori_loop(0, n_chunks // 2, body, 0)

        @pl.when(lax.rem(n_chunks, 2) == 1)
        def _():
            consume(last, buf_a)

    two_chunk_pipeline(idx_dots, idx_keys, da_ref, db_ref)

    def count(pred, trips, width):
        tile = 2 * SUBLANES
        lanes_acc = 4

        def body(c, accs):
            ks = c * width if isinstance(c, int) else pl.multiple_of(c * width, width)
            mb = jnp.where(pred(ks, width), jnp.ones((), bf16), jnp.zeros((), bf16))
            m3 = mb.reshape(width // tile, tile, LANES)
            accs = list(accs)
            for i in range(width // tile):
                accs[i % lanes_acc] = accs[i % lanes_acc] + m3[i]
            return tuple(accs)

        accs = (jnp.zeros((tile, LANES), bf16),) * lanes_acc
        if isinstance(trips, int):
            for c in range(trips):
                accs = body(c, accs)
        else:
            accs = lax.fori_loop(0, trips, body, accs)
        tot = (accs[0].astype(f32) + accs[1].astype(f32)) + (accs[2].astype(f32) + accs[3].astype(f32))
        return jnp.sum(tot, axis=0, keepdims=True)

    def select(n_wide):
        def search16(ref, target):
            def cnt_ge(cand):
                c16 = cand.astype(i16)
                return count(lambda ks, w: ref[pl.ds(ks, w), :] >= c16, n_wide, COUNT_CHUNK)

            zero = jnp.zeros((1, LANES), jnp.int32)
            c0 = cnt_ge(zero)
            ok0 = c0 >= target
            t0 = jnp.where(ok0, zero, jnp.full((1, LANES), MIN16, jnp.int32))
            ct0 = jnp.where(ok0, c0, jnp.full((1, LANES), 1e9, f32))

            def body(i, st):
                t, ct = st
                cand = t | lax.shift_left(jnp.int32(1), 14 - i)
                c = cnt_ge(cand)
                ok = c >= target
                return jnp.where(ok, cand, t), jnp.where(ok, c, ct)

            return lax.fori_loop(0, 15, body, (t0, ct0))

        t_hi, _ = search16(hi_ref, jnp.full((1, LANES), kf, f32))
        thi16 = t_hi.astype(i16)
        need2 = kf - count(lambda ks, w: hi_ref[pl.ds(ks, w), :] > thi16, n_wide, COUNT_CHUNK)
        for c in range(n_wide):
            rs = pl.ds(c * COUNT_CHUNK, COUNT_CHUNK)
            lom_ref[rs, :] = jnp.where(hi_ref[rs, :] == thi16, lo_ref[rs, :], jnp.full((), MIN16, i16))
        t_lo, ct = search16(lom_ref, need2)
        return t_hi, t_lo, need2, ct

    n_wide = (n_chunks * KEY_CHUNK + COUNT_CHUNK - 1) // COUNT_CHUNK
    t_hi, t_lo, need2, ct = lax.switch(
        n_wide - 1, [functools.partial(select, k) for k in range(1, SEQ // COUNT_CHUNK + 1)])
    thi16 = t_hi.astype(i16)
    tlo16 = t_lo.astype(i16)
    settled = jnp.logical_or(ct == need2, t_hi == MIN16)
    has_tie = jnp.max(jnp.where(settled, 0.0, 1.0)) > 0.0

    def tie_search():
        need3 = need2 - count(lambda ks, w: lom_ref[pl.ds(ks, w), :] > tlo16, n_chunks, KEY_CHUNK)
        big = jnp.full((), 32767, i16)

        def mk_pos(c, carry):
            ks = pl.multiple_of(c * KEY_CHUNK, KEY_CHUNK)
            pos = (ks + krow_full).astype(i16)
            tie_pos = jnp.where(lo_ref[pl.ds(ks, KEY_CHUNK), :] == tlo16, pos, big)
            lom_ref[pl.ds(ks, KEY_CHUNK), :] = jnp.where(hi_ref[pl.ds(ks, KEY_CHUNK), :] == thi16, tie_pos, big)
            return carry

        lax.fori_loop(0, n_chunks, mk_pos, 0)

        def jbit(i, j):
            cand = j | lax.shift_left(jnp.int32(1), 12 - i)
            c16 = cand.astype(i16)
            c = count(lambda ks, w: lom_ref[pl.ds(ks, w), :] < c16, n_chunks, KEY_CHUNK)
            return jnp.where(c <= need3, cand, j)

        return lax.fori_loop(0, 13, jbit, jnp.zeros((1, LANES), jnp.int32))

    jsel = lax.cond(has_tie, tie_search, lambda: jnp.full((1, LANES), 2 * SEQ, jnp.int32))
    jsel16 = jsel.astype(i16)

    low = lax.broadcasted_iota(jnp.int32, (1, LANES), 1) < HEAD_DIM
    pairs = N_HEADS // 2
    m_ref[...] = jnp.full(m_ref.shape, NEG_BIG, f32)
    acc_ref[...] = jnp.zeros_like(acc_ref)
    q = q_ref[...]
    qpair = []
    for j in range(pairs):
        blk = q[:, j * LANES:(j + 1) * LANES]
        nil = jnp.zeros_like(blk)
        qpair.append(jnp.concatenate([jnp.where(low, blk, nil), jnp.where(low, nil, blk)], axis=0))
    open16 = jnp.zeros((), bf16)
    shut16 = jnp.full((), NEG_BIG, bf16)

    def qk_all(c, lg_ref):
        ks = pl.multiple_of(c * KEY_CHUNK, KEY_CHUNK)
        for j in range(pairs):
            lg_ref[j] = _dot_nt(qpair[j], k_ref[pl.ds(ks, KEY_CHUNK), j * LANES:(j + 1) * LANES])

    def soft_pv(c, lg_ref):
        ks = pl.multiple_of(c * KEY_CHUNK, KEY_CHUNK)
        hk = hi_ref[pl.ds(ks, KEY_CHUNK), :]
        lk = lo_ref[pl.ds(ks, KEY_CHUNK), :]
        pos = (ks + krow_full).astype(i16)
        b = jnp.where(lk == tlo16, jnp.where(pos < jsel16, open16, shut16), shut16)
        b = jnp.where(hk == thi16, jnp.where(lk > tlo16, open16, b), shut16)
        b = jnp.where(hk > thi16, open16, b)
        b = jnp.where(hk == MIN16, shut16, b)
        bias = b.astype(f32).T
        bias2 = jnp.concatenate([bias, bias], axis=0)
        for j in range(pairs):
            lg = lg_ref[j] + bias2
            m_old = m_ref[j]
            m_new = jnp.maximum(m_old, jnp.max(lg, axis=1, keepdims=True))
            p = jnp.exp2(lg - m_new).astype(bf16)
            a = jnp.exp2(m_old - m_new)
            pv = _dot(p, va_ref[pl.ds(ks, KEY_CHUNK), 2 * j * LANES:(2 * j + 2) * LANES])
            own = jnp.concatenate([pv[0:Q_BLOCK, 0:LANES], pv[Q_BLOCK:2 * Q_BLOCK, LANES:2 * LANES]], axis=0)
            acc_ref[j] = a * acc_ref[j] + own
            m_ref[j] = m_new

    two_chunk_pipeline(qk_all, soft_pv, lga_ref, lgb_ref)

    for j in range(pairs):
        acc = acc_ref[j]
        out = acc / pltpu.roll(acc, HEAD_DIM, 1)
        o_ref[:, j * LANES:(j + 1) * LANES] = jnp.where(low, out[0:Q_BLOCK], out[Q_BLOCK:2 * Q_BLOCK]).astype(bf16)


def _dsa_prompt(qb, qit, wit, kb, va, kib):
    nq = SEQ // Q_BLOCK
    qrow = lambda c: pl.BlockSpec((None, Q_BLOCK, c), lambda b, t: (b, t, 0))
    qcol = lambda r: pl.BlockSpec((None, r, Q_BLOCK), lambda b, t: (b, 0, t))
    seq = lambda c: pl.BlockSpec((None, SEQ, c), lambda b, t: (b, 0, 0))
    return pl.pallas_call(
        _dsa_prompt_kernel,
        grid=(BATCH, nq),
        in_specs=[qrow(ATT_W), qcol(IDX_HEADS * IDX_DIM), qcol(IDX_HEADS), seq(ATT_W), seq(2 * ATT_W),
                  seq(IDX_DIM)],
        out_specs=qrow(ATT_W),
        out_shape=jax.ShapeDtypeStruct((BATCH, SEQ, ATT_W), bf16),
        scratch_shapes=[pltpu.VMEM((SEQ, Q_BLOCK), jnp.int16),
                        pltpu.VMEM((SEQ, Q_BLOCK), jnp.int16),
                        pltpu.VMEM((SEQ, Q_BLOCK), jnp.int16),
                        pltpu.VMEM((N_HEADS // 2, 2 * Q_BLOCK, 1), f32),
                        pltpu.VMEM((N_HEADS // 2, 2 * Q_BLOCK, LANES), f32),
                        pltpu.VMEM((N_HEADS // 2, 2 * Q_BLOCK, KEY_CHUNK), f32),
                        pltpu.VMEM((N_HEADS // 2, 2 * Q_BLOCK, KEY_CHUNK), f32),
                        pltpu.VMEM((KEY_CHUNK, IDX_HEADS * Q_BLOCK), f32),
                        pltpu.VMEM((KEY_CHUNK, IDX_HEADS * Q_BLOCK), f32)],
        compiler_params=pltpu.CompilerParams(dimension_semantics=("arbitrary", "arbitrary"),
                                             vmem_limit_bytes=VMEM_LIMIT),
        name="dsa_prompt",
    )(qb, qit, wit, kb, va, kib)


S_KEYS = PAST_LEN + PAGE_SIZE
BITS_PER_PASS = 4
SEQ_PER_STEP = 2


def _dsa_sample_kernel(pt_ref, q_ref, qi_ref, wi_ref, kin_ref, kn_ref, vn_ref, *rest):
    g_n = SEQ_PER_STEP
    kidx_refs = rest[0:g_n * N_PAGES]
    k_refs = rest[g_n * N_PAGES:2 * g_n * N_PAGES]
    v_refs = rest[2 * g_n * N_PAGES:3 * g_n * N_PAGES]
    o_ref = rest[3 * g_n * N_PAGES]
    del pt_ref
    rows = DEC_SEQ * IDX_HEADS
    kf = float(TOPK)
    kpos = lax.broadcasted_iota(jnp.int32, (1, S_KEYS), 1)
    tq = lax.broadcasted_iota(jnp.int32, (DEC_SEQ, 1), 0)
    zero = jnp.zeros((DEC_SEQ, 1), jnp.int32)

    def pad_keys(x):
        return jnp.concatenate([x, jnp.zeros((PAGE_SIZE - x.shape[0], x.shape[1]), x.dtype)], axis=0)

    def count(pred):
        return jnp.sum(jnp.where(pred, 1.0, 0.0), axis=1, keepdims=True)

    def threshold(g):
        qi = qi_ref[g].astype(bf16)
        wi = wi_ref[g]

        def idx_score(d):
            r = jnp.maximum(d, 0.0) * wi
            return jnp.sum(r.reshape(DEC_SEQ, IDX_HEADS, PAGE_SIZE), axis=1)

        parts = [idx_score(_dot(qi, kidx_refs[g * N_PAGES + j][...].astype(bf16))) for j in range(N_PAGES)]
        parts.append(idx_score(_dot_nt(qi, pad_keys(kin_ref[g]).astype(bf16))))
        s = jnp.concatenate(parts, axis=1)
        keys = jnp.where((kpos - PAST_LEN) <= tq, _order_key(s), INT_MIN)
        c0 = count(keys >= zero)
        thr = jnp.where(c0 >= kf, zero, jnp.full((DEC_SEQ, 1), INT_MIN, jnp.int32))
        cthr = jnp.where(c0 >= kf, c0, 1e9)
        shift = 31
        while shift > 0:
            nb = min(BITS_PER_PASS, shift)
            shift -= nb
            num = zero
            for c in range(1, 2 ** nb):
                cnt = count(keys >= (thr | (c << shift)))
                ok = cnt >= kf
                num = num + ok.astype(jnp.int32)
                cthr = jnp.where(ok, cnt, cthr)
            thr = thr | lax.shift_left(num, shift)
        need = kf - count(keys > thr)
        settled = jnp.logical_or(cthr == kf, thr == INT_MIN)
        return keys, thr, need, jnp.max(jnp.where(settled, 0.0, 1.0))

    sel_in = [threshold(g) for g in range(g_n)]
    unsettled = sel_in[0][3]
    for g in range(1, g_n):
        unsettled = jnp.maximum(unsettled, sel_in[g][3])

    def tie_search():
        out = []
        for keys, thr, need, _ in sel_in:
            def jbit(i, j, keys=keys, thr=thr, need=need):
                cand = j | lax.shift_left(jnp.int32(1), 12 - i)
                c = count(jnp.logical_and(keys == thr, kpos < cand))
                return jnp.where(c <= need, cand, j)
            out.append(lax.fori_loop(0, 13, jbit, zero))
        return tuple(out)

    no_tie = tuple(jnp.full((DEC_SEQ, 1), 2 * S_KEYS, jnp.int32) for _ in range(g_n))
    jsels = lax.cond(unsettled > 0.0, tie_search, lambda: no_tie)

    hrow = lax.broadcasted_iota(jnp.int32, (rows, ATT_W), 0) % N_HEADS
    hcol = lax.broadcasted_iota(jnp.int32, (rows, ATT_W), 1) // HEAD_DIM
    own = hrow == hcol
    for g in range(g_n):
        keys, thr, _, _ = sel_in[g]
        sel = jnp.logical_and(
            jnp.logical_or(keys > thr, jnp.logical_and(keys == thr, kpos < jsels[g])), keys != INT_MIN)
        bias4 = jnp.where(sel, 0.0, NEG_BIG)
        bias = jnp.broadcast_to(bias4[:, None, :], (DEC_SEQ, N_HEADS, S_KEYS)).reshape(rows, S_KEYS)

        q4 = q_ref[g] * ATT_SCALE
        q32 = jnp.broadcast_to(q4[:, None, :], (DEC_SEQ, N_HEADS, ATT_W)).reshape(rows, ATT_W)
        qbd = jnp.where(own, q32, 0.0).astype(bf16)

        kn = pad_keys(kn_ref[g]).astype(bf16)
        vn = pad_keys(vn_ref[g]).astype(bf16)
        lg = jnp.concatenate([_dot(qbd, k_refs[g * N_PAGES + j][...].astype(bf16)) for j in range(N_PAGES)]
                             + [_dot_nt(qbd, kn)], axis=1) + bias
        m = jnp.max(lg, axis=1, keepdims=True)
        p = jnp.exp(lg - m)
        l = jnp.sum(p, axis=1, keepdims=True)
        pb = p.astype(bf16)
        pv = _dot(pb[:, PAST_LEN:], vn)
        for j in range(N_PAGES):
            pv = pv + _dot_nt(pb[:, j * PAGE_SIZE:(j + 1) * PAGE_SIZE], v_refs[g * N_PAGES + j][...].astype(bf16))
        out = jnp.where(own, pv, 0.0) / l
        o_ref[g] = jnp.sum(out.reshape(DEC_SEQ, N_HEADS, ATT_W), axis=1)


def _dsa_sample(page_table, q, qi, wi, ki_new, k_new, v_new, kidx_pages, k_pages, v_pages, layer):
    n_pool = kidx_pages.shape[0] // (DEPTH // 2)
    rows = DEC_SEQ * IDX_HEADS
    g_n = SEQ_PER_STEP
    per_seq = lambda r, c: pl.BlockSpec((g_n, r, c), lambda s, pt: (s, 0, 0))

    def page(r, g, j):
        return pl.BlockSpec((None, r, PAGE_SIZE), lambda s, pt: (layer * n_pool + pt[s * g_n + g, j], 0, 0))

    pages = lambda r: [page(r, g, j) for g in range(g_n) for j in range(N_PAGES)]
    in_specs = [per_seq(DEC_SEQ, ATT_W), per_seq(rows, IDX_DIM), per_seq(rows, 1),
                per_seq(SUBLANES, IDX_DIM), per_seq(SUBLANES, ATT_W), per_seq(SUBLANES, ATT_W)]
    in_specs += pages(IDX_DIM) + pages(ATT_W) + pages(ATT_W)
    grid_spec = pltpu.PrefetchScalarGridSpec(
        num_scalar_prefetch=1, grid=(DEC_BATCH // g_n,), in_specs=in_specs,
        out_specs=per_seq(DEC_SEQ, ATT_W))
    n_in = g_n * N_PAGES
    return pl.pallas_call(
        _dsa_sample_kernel,
        grid_spec=grid_spec,
        out_shape=jax.ShapeDtypeStruct((DEC_BATCH, DEC_SEQ, ATT_W), f32),
        compiler_params=pltpu.CompilerParams(dimension_semantics=("arbitrary",),
                                             vmem_limit_bytes=VMEM_LIMIT),
        name="dsa_sample",
    )(page_table, q, qi, wi, ki_new, k_new, v_new,
      *([kidx_pages] * n_in), *([k_pages] * n_in), *([v_pages] * n_in))


def _rope_tables(pos):
    half = ROT // 2
    inv = jnp.power(jnp.float32(ROPE_THETA), -jnp.arange(half, dtype=f32) * 2.0 / ROT)
    ang = pos.astype(f32)[:, None] * inv[None, :]
    cos, sin = jnp.cos(ang), jnp.sin(ang)
    t = pos.shape[0]
    rest = HEAD_DIM - ROT
    c64 = jnp.concatenate([cos, cos, jnp.ones((t, rest), f32)], axis=1)
    s1 = jnp.concatenate([jnp.zeros((t, half), f32), sin, jnp.zeros((t, rest), f32)], axis=1)
    s2 = jnp.concatenate([-sin, jnp.zeros((t, half + rest), f32)], axis=1)
    two = lambda a: jnp.concatenate([a, a], axis=1)
    return two(c64), two(s1), two(s2)


def _to_tm(a):
    return a.transpose(1, 0, 2).reshape(a.shape[1] * a.shape[0], a.shape[2])


def _from_tm(a, t=DEC_SEQ):
    return a.reshape(t, DEC_BATCH, a.shape[-1]).transpose(1, 0, 2)


def kernel(x_prompt, x_sample, state_b_conv, state_c_pool, cache_k, cache_v, cache_kidx, state_ffn_conv, page_table, w_in_even, a_ln_g, a_ln_b, a_ws, a_bs, b_conv_w, w_out_even, w_in_odd, c_w, c_scale, w_out_odd, ln_mix_g, ln_mix_b, ffn_w_up, ffn_conv_w, ffn_conv_b, ffn_w_down, ln_ffn_g, ln_ffn_b):
    row = lambda a: a.reshape(1, -1)
    xp = x_prompt
    xs = _to_tm(x_sample)
    outs = {}
    ffn_p, ffn_s = [], []

    rows3 = lambda a: a[:, None, :]
    ffn_wts = (rows3(ln_mix_g), rows3(ln_mix_b), ffn_w_up.astype(bf16), ffn_conv_w, rows3(ffn_conv_b),
               ffn_w_down.astype(bf16), rows3(ln_ffn_g), rows3(ln_ffn_b))

    def post(layer, xp, xs, cap, cbp, cas, cbs, wo):
        wob = wo.astype(bf16)
        xp, st_p = _post_prompt(layer, xp, cap, cbp, wob, *ffn_wts)
        xs, st_s = _post_sample(layer, xs, cas, cbs, wob, *ffn_wts, _to_tm(state_ffn_conv[layer]))
        ffn_p.append(st_p[:, SUBLANES - (CONV_W - 1):])
        ffn_s.append(_from_tm(st_s, CONV_W - 1))
        return xp, xs

    for layer in range(DEPTH):
        if layer % 2 == 0:
            e = layer // 2
            w = w_in_even[e].astype(bf16)
            lg, lb = row(a_ln_g[e]), row(a_ln_b[e])
            bias = jnp.repeat(a_bs[e].T, CHUNK, axis=1)
            wcoef = jnp.repeat(a_ws[e][:, :DEC_SEQ, :DEC_SEQ].transpose(1, 2, 0).reshape(DEC_SEQ * DEC_SEQ, A_GROUPS),
                               CHUNK, axis=1)
            a_p, b_p, bst_p = _even_pre_prompt(xp, w, lg, lb, a_ws[e], bias, b_conv_w[e])
            a_s, b_s, v_s, e_s = _even_pre_sample(xs, w, lg, lb, wcoef, bias, b_conv_w[e],
                                                  _to_tm(state_b_conv[e]))
            outs.setdefault("a_v_s", []).append(_from_tm(v_s))
            outs.setdefault("b_p", []).append(bst_p[:, SUBLANES - (CONV_W - 1):])
            outs.setdefault("b_s", []).append(_from_tm(e_s)[:, DEC_SEQ - (CONV_W - 1):])
            xp, xs = post(layer, xp, xs, a_p, b_p, a_s, b_s, w_out_even[e])
        else:
            o = layer // 2
            wm = w_in_odd[o][:, :O_KI].astype(bf16)
            wki = w_in_odd[o][:, O_KI:O_WI]
            wwi = w_in_odd[o][:, O_WI:]
            wt = jnp.concatenate([wki, wki, wwi, jnp.zeros((D_MODEL, LANES - IDX_HEADS), f32)],
                                 axis=1).astype(bf16)
            cwt = c_w[o].astype(bf16)
            cs = row(c_scale[o])
            tabs_p = _rope_tables(jnp.arange(SEQ, dtype=jnp.int32))
            pos_s = PAST_LEN + jnp.repeat(jnp.arange(DEC_SEQ, dtype=jnp.int32), DEC_BATCH)
            tabs_s = _rope_tables(pos_s)
            (k_p, v_p, ki_p, qb, kb, va, qit, kib, wit, co_p, ph_p) = _odd_pre_prompt(
                xp, wm, wt, *tabs_p, cwt, cs)
            att_p = _dsa_prompt(qb, qit, wit, kb, va, kib)
            (k_s, v_s2, ki_s, q_s, qi_s, wi_s, co_s, p_s) = _odd_pre_sample(
                xs, wm, wt, *tabs_s, cwt, cs, _to_tm(state_c_pool[o]))
            rows = DEC_SEQ * IDX_HEADS
            pad8 = lambda a: jnp.pad(_from_tm(a), ((0, 0), (0, SUBLANES - DEC_SEQ), (0, 0)))
            page_t = lambda c, w: jnp.moveaxis(c, 2, -1).reshape(-1, w, PAGE_SIZE)
            att_s = _dsa_sample(
                page_table,
                _from_tm(q_s),
                _from_tm(qi_s).reshape(DEC_BATCH, rows, IDX_DIM),
                _from_tm(wi_s)[:, :, :IDX_HEADS].reshape(DEC_BATCH, rows, 1),
                pad8(ki_s), pad8(k_s), pad8(v_s2),
                page_t(cache_kidx, IDX_DIM), page_t(cache_k, ATT_W), page_t(cache_v, ATT_W), o)
            heads = lambda a: a.reshape(a.shape[0], a.shape[1], N_HEADS, HEAD_DIM)
            outs.setdefault("c_p", []).append(ph_p[:, 2 * SUBLANES - POOL_HIST:])
            outs.setdefault("c_s", []).append(
                jnp.concatenate([state_c_pool[o][:, DEC_SEQ:], _from_tm(p_s)], axis=1))
            outs.setdefault("k_p", []).append(heads(k_p))
            outs.setdefault("v_p", []).append(heads(v_p))
            outs.setdefault("ki_p", []).append(ki_p)
            outs.setdefault("k_s", []).append(heads(_from_tm(k_s)))
            outs.setdefault("v_s", []).append(heads(_from_tm(v_s2)))
            outs.setdefault("ki_s", []).append(_from_tm(ki_s))
            xp, xs = post(layer, xp, xs, co_p, att_p, co_s, _to_tm(att_s).astype(bf16), w_out_odd[o])

    st = lambda name: jnp.stack(outs[name])
    return (xp, _from_tm(xs), st("a_v_s"), st("b_p"), st("b_s"), st("c_p"), st("c_s"),
            st("k_p"), st("v_p"), st("ki_p"), st("k_s"), st("v_s"), st("ki_s"),
            jnp.stack(ffn_p), jnp.stack(ffn_s))
```

```python
import functools

import jax
import jax.numpy as jnp
import numpy as np
from jax import lax
from jax.experimental import pallas as pl
from jax.experimental.pallas import tpu as pltpu

D_MODEL = 1024
BATCH = 4
SEQ = 4096
DEPTH = 2
DEC_BATCH = 128
DEC_SEQ = 4
PAST_LEN = 2048
PAGE_SIZE = 128
N_PAGES = PAST_LEN // PAGE_SIZE

A_W = D_MODEL // 2
B_W = D_MODEL // 2
C_W = D_MODEL // 2
ATT_W = D_MODEL // 2
CHUNK = 128
A_GROUPS = A_W // CHUNK
CONV_W = 3
POOL_WINDOWS = (2, 4, 8, 16)
C_GROUPS = len(POOL_WINDOWS)
C_GROUP_W = C_W // C_GROUPS
POOL_HIST = max(POOL_WINDOWS) - 1
HEAD_DIM = 64
N_HEADS = ATT_W // HEAD_DIM
IDX_HEADS = 8
IDX_DIM = 64
TOPK = 256
Q_BLOCK = 128
ROPE_THETA = 500000.0
ROT = HEAD_DIM // 4
D_FF = ((8 * D_MODEL // 3 + 127) // 128) * 128
ALPHA = (2.0 * DEPTH) ** 0.25
LN_EPS = 1e-5
ATT_SCALE = HEAD_DIM ** -0.5
LOG2E = 1.4426950408889634

E_U, E_V, E_B, E_C, E_H = 0, A_W, 2 * A_W, 2 * A_W + B_W, 2 * A_W + 2 * B_W
O_P, O_Q, O_K, O_V, O_QI = 0, C_W, C_W + ATT_W, C_W + 2 * ATT_W, C_W + 3 * ATT_W
O_KI = O_QI + IDX_HEADS * IDX_DIM
O_WI = O_KI + IDX_DIM

LANES = 128
SUBLANES = 8
VMEM_LIMIT = 56 * 1024 * 1024

TM = 512
TM_POST = 256
FF_CHUNK = 256
KEY_CHUNK = 512
SCAN_CHUNK = 256
S_ROWS = DEC_SEQ * DEC_BATCH
INT_MIN = -2 ** 31
NEG_BIG = -1e30

bf16 = jnp.bfloat16
f32 = jnp.float32


def _ln(x, g, b):
    mu = jnp.mean(x, axis=-1, keepdims=True)
    xc = x - mu
    var = jnp.mean(xc * xc, axis=-1, keepdims=True)
    return xc * lax.rsqrt(var + LN_EPS) * g + b


def _gelu(x):
    c = (2.0 / jnp.pi) ** 0.5
    return x * (0.5 * (1.0 + jnp.tanh(c * (x + 0.044715 * (x * x * x)))))


def _dot(a, b):
    return jnp.dot(a, b, preferred_element_type=f32)


def _dot_nt(a, b):
    return lax.dot_general(a, b, (((1,), (1,)), ((), ())), preferred_element_type=f32)


def _prev_rows_prompt(cur, halo):
    ext = jnp.concatenate([halo, cur], axis=0)
    p1 = pltpu.roll(ext, 1, 0)[SUBLANES:]
    p2 = pltpu.roll(ext, 2, 0)[SUBLANES:]
    return p1, p2


def _prev_rows_sample(cur, hist):
    n = DEC_BATCH
    p1 = jnp.concatenate([hist[n:2 * n], cur[0:3 * n]], axis=0)
    p2 = jnp.concatenate([hist, cur[0:2 * n]], axis=0)
    return p1, p2


def _causal_conv(cur, p1, p2, w_ref, c0, width):
    return (cur * w_ref[2:3, c0:c0 + width] + p1 * w_ref[1:2, c0:c0 + width]
            + p2 * w_ref[0:1, c0:c0 + width])


def _even_pre_kernel(*refs, sample):
    if sample:
        (x_ref, w_ref, lg_ref, lb_ref, wcoef_ref, bias_ref, cw_ref, hist_ref,
         a_ref, b_ref, v_ref, e_ref) = refs
    else:
        (x_ref, w_ref, lg_ref, lb_ref, ws_ref, bias_ref, cw_ref,
         a_ref, b_ref, st_ref, halo_ref) = refs

        @pl.when(pl.program_id(1) == 0)
        def _():
            halo_ref[...] = jnp.zeros_like(halo_ref)

    xb = x_ref[...].astype(bf16)
    rows = xb.shape[0]

    def proj(c0):
        return _dot(xb, w_ref[:, c0:c0 + A_W])

    u = _gelu(proj(E_U))
    v = _ln(_gelu(proj(E_V)), lg_ref[...], lb_ref[...])
    if sample:
        v_ref[...] = v
        n = DEC_BATCH
        for t in range(DEC_SEQ):
            mix = bias_ref[t:t + 1, :]
            for s in range(t + 1):
                r = t * DEC_SEQ + s
                mix = mix + wcoef_ref[r:r + 1, :] * v[s * n:(s + 1) * n]
            a_ref[t * n:(t + 1) * n, :] = (u[t * n:(t + 1) * n] * mix).astype(bf16)
    else:
        vb = v.astype(bf16)
        ri = lax.broadcasted_iota(jnp.int32, (CHUNK, CHUNK), 0)
        ci = lax.broadcasted_iota(jnp.int32, (CHUNK, CHUNK), 1)
        for g in range(A_GROUPS):
            gs = slice(g * CHUNK, (g + 1) * CHUNK)
            wc = jnp.where(ri >= ci, ws_ref[g], 0.0).astype(bf16)
            for c in range(rows // CHUNK):
                rs = slice(c * CHUNK, (c + 1) * CHUNK)
                mix = _dot(wc, vb[rs, gs]) + bias_ref[:, gs]
                a_ref[rs, gs] = (u[rs, gs] * mix).astype(bf16)

    e = proj(E_C) * proj(E_H)
    if sample:
        e_ref[...] = e
        p1, p2 = _prev_rows_sample(e, hist_ref[...])
    else:
        p1, p2 = _prev_rows_prompt(e, halo_ref[...])
        halo_ref[...] = e[rows - SUBLANES:]
        st_ref[...] = e[rows - SUBLANES:]
    conv = _causal_conv(e, p1, p2, cw_ref, 0, B_W)
    b_ref[...] = (proj(E_B) * conv).astype(bf16)


def _full(shape):
    nd = len(shape)
    return pl.BlockSpec(shape, lambda *_: (0,) * nd)


def _even_pre_prompt(x, w, lg, lb, ws, bias, cw):
    nt = SEQ // TM
    row = lambda c: pl.BlockSpec((None, TM, c), lambda b, t: (b, t, 0))
    return pl.pallas_call(
        functools.partial(_even_pre_kernel, sample=False),
        grid=(BATCH, nt),
        in_specs=[row(D_MODEL), _full(w.shape), _full(lg.shape), _full(lb.shape), _full(ws.shape),
                  _full(bias.shape), _full(cw.shape)],
        out_specs=[row(A_W), row(B_W), pl.BlockSpec((None, SUBLANES, B_W), lambda b, t: (b, 0, 0))],
        out_shape=[jax.ShapeDtypeStruct((BATCH, SEQ, A_W), bf16),
                   jax.ShapeDtypeStruct((BATCH, SEQ, B_W), bf16),
                   jax.ShapeDtypeStruct((BATCH, SUBLANES, B_W), f32)],
        scratch_shapes=[pltpu.VMEM((SUBLANES, B_W), f32)],
        compiler_params=pltpu.CompilerParams(dimension_semantics=("arbitrary", "arbitrary"),
                                             vmem_limit_bytes=VMEM_LIMIT),
        name="even_pre_prompt",
    )(x, w, lg, lb, ws, bias, cw)


def _even_pre_sample(x, w, lg, lb, wcoef, bias, cw, hist):
    args = (x, w, lg, lb, wcoef, bias, cw, hist)
    return pl.pallas_call(
        functools.partial(_even_pre_kernel, sample=True),
        grid=(1,),
        in_specs=[_full(a.shape) for a in args],
        out_specs=[_full((S_ROWS, A_W)), _full((S_ROWS, B_W)), _full((S_ROWS, A_W)), _full((S_ROWS, B_W))],
        out_shape=[jax.ShapeDtypeStruct((S_ROWS, A_W), bf16),
                   jax.ShapeDtypeStruct((S_ROWS, B_W), bf16),
                   jax.ShapeDtypeStruct((S_ROWS, A_W), f32),
                   jax.ShapeDtypeStruct((S_ROWS, B_W), f32)],
        compiler_params=pltpu.CompilerParams(dimension_semantics=("arbitrary",),
                                             vmem_limit_bytes=VMEM_LIMIT),
        name="even_pre_sample",
    )(*args)


def _post_kernel(*refs, sample):
    if sample:
        (x_ref, ca_ref, cb_ref, wo_ref, g1_ref, b1_ref, wu_ref, cw_ref, cb2_ref, wd_ref, g2_ref, b2_ref,
         hist_ref, o_ref, st_ref, acc_ref) = refs
    else:
        (x_ref, ca_ref, cb_ref, wo_ref, g1_ref, b1_ref, wu_ref, cw_ref, cb2_ref, wd_ref, g2_ref, b2_ref,
         o_ref, st_ref, acc_ref, halo_ref) = refs

        @pl.when(pl.program_id(1) == 0)
        def _():
            halo_ref[...] = jnp.zeros_like(halo_ref)

    half = ca_ref.shape[-1]
    y = _dot(ca_ref[...], wo_ref[0:half, :]) + _dot(cb_ref[...], wo_ref[half:2 * half, :])
    x1 = _ln(ALPHA * x_ref[...] + y, g1_ref[...], b1_ref[...])
    x1b = x1.astype(bf16)
    rows = x1b.shape[0]

    def up(c0):
        return (_dot(x1b, wu_ref[:, c0:c0 + FF_CHUNK]), _dot(x1b, wu_ref[:, D_FF + c0:D_FF + c0 + FF_CHUNK]))

    def conv_half(hu, c0):
        if sample:
            p1, p2 = _prev_rows_sample(hu, hist_ref[:, c0:c0 + FF_CHUNK])
            st_ref[:, c0:c0 + FF_CHUNK] = hu[2 * DEC_BATCH:]
        else:
            p1, p2 = _prev_rows_prompt(hu, halo_ref[:, c0:c0 + FF_CHUNK])
            halo_ref[:, c0:c0 + FF_CHUNK] = hu[rows - SUBLANES:]
            st_ref[:, c0:c0 + FF_CHUNK] = hu[rows - SUBLANES:]
        return _causal_conv(hu, p1, p2, cw_ref, c0, FF_CHUNK) + cb2_ref[:, c0:c0 + FF_CHUNK]

    n_ff = D_FF // FF_CHUNK
    nxt = up(0)
    for c in range(n_ff):
        c0 = c * FF_CHUNK
        cur = nxt
        if c + 1 < n_ff:
            nxt = up(c0 + FF_CHUNK)
        hg = conv_half(cur[0], c0)
        hv = conv_half(cur[1], D_FF + c0)
        act = (hg * jax.nn.sigmoid(hg) * hv).astype(bf16)
        part = _dot(act, wd_ref[c0:c0 + FF_CHUNK, :])
        if c == 0:
            acc_ref[...] = part
        else:
            acc_ref[...] += part
    o_ref[...] = _ln(ALPHA * x1 + acc_ref[...], g2_ref[...], b2_ref[...])


def _layer_spec(a, layer, **kw):
    return pl.BlockSpec((None,) + a.shape[1:], lambda *_: (layer,) + (0,) * (a.ndim - 1), **kw)


def _post_prompt(layer, x, ca, cb, wo, g1, b1, wu, cw, cb2, wd, g2, b2):
    nt = SEQ // TM_POST
    row = lambda c: pl.BlockSpec((None, TM_POST, c), lambda b, t: (b, t, 0))
    const = lambda a: pl.BlockSpec(a.shape, lambda b, t: (0,) * a.ndim, pipeline_mode=pl.Buffered(1))
    stacked = lambda a: _layer_spec(a, layer, pipeline_mode=pl.Buffered(1))
    return pl.pallas_call(
        functools.partial(_post_kernel, sample=False),
        grid=(BATCH, nt),
        in_specs=[row(D_MODEL), row(ca.shape[-1]), row(cb.shape[-1]), const(wo)]
                 + [stacked(a) for a in (g1, b1, wu, cw, cb2, wd, g2, b2)],
        out_specs=[row(D_MODEL), pl.BlockSpec((None, SUBLANES, 2 * D_FF), lambda b, t: (b, 0, 0))],
        out_shape=[jax.ShapeDtypeStruct((BATCH, SEQ, D_MODEL), f32),
                   jax.ShapeDtypeStruct((BATCH, SUBLANES, 2 * D_FF), f32)],
        scratch_shapes=[pltpu.VMEM((TM_POST, D_MODEL), f32), pltpu.VMEM((SUBLANES, 2 * D_FF), f32)],
        compiler_params=pltpu.CompilerParams(dimension_semantics=("arbitrary", "arbitrary"),
                                             vmem_limit_bytes=VMEM_LIMIT),
        name="post_prompt",
    )(x, ca, cb, wo, g1, b1, wu, cw, cb2, wd, g2, b2)


def _post_sample(layer, x, ca, cb, wo, g1, b1, wu, cw, cb2, wd, g2, b2, hist):
    args = (x, ca, cb, wo, g1, b1, wu, cw, cb2, wd, g2, b2, hist)
    full = lambda a: _full(a.shape)
    stacked = lambda a: _layer_spec(a, layer)
    return pl.pallas_call(
        functools.partial(_post_kernel, sample=True),
        grid=(1,),
        in_specs=[full(x), full(ca), full(cb), full(wo)]
                 + [stacked(a) for a in (g1, b1, wu, cw, cb2, wd, g2, b2)] + [full(hist)],
        out_specs=[_full((S_ROWS, D_MODEL)), _full((2 * DEC_BATCH, 2 * D_FF))],
        out_shape=[jax.ShapeDtypeStruct((S_ROWS, D_MODEL), f32),
                   jax.ShapeDtypeStruct((2 * DEC_BATCH, 2 * D_FF), f32)],
        scratch_shapes=[pltpu.VMEM((S_ROWS, D_MODEL), f32)],
        compiler_params=pltpu.CompilerParams(dimension_semantics=("arbitrary",),
                                             vmem_limit_bytes=VMEM_LIMIT),
        name="post_sample",
    )(*args)


def _rope(blk, cos, s1, s2):
    return blk * cos + pltpu.roll(blk, ROT // 2, 1) * s1 + pltpu.roll(blk, LANES - ROT // 2, 1) * s2


def _odd_pre_kernel(*refs, sample):
    if sample:
        (x_ref, wm_ref, wt_ref, cos_ref, s1_ref, s2_ref, cwt_ref, cs_ref, hist_ref,
         k_ref, v_ref, ki_ref, q_ref, qi_ref, wi_ref, co_ref, p_ref) = refs
    else:
        (x_ref, wm_ref, wt_ref, cos_ref, s1_ref, s2_ref, cwt_ref, cs_ref,
         k_ref, v_ref, ki_ref, qb_ref, kb_ref, va_ref, qit_ref, kib_ref, wit_ref, co_ref, ph_ref,
         halo_ref) = refs

        @pl.when(pl.program_id(1) == 0)
        def _():
            halo_ref[...] = jnp.zeros_like(halo_ref)

    xb = x_ref[...].astype(bf16)
    rows = xb.shape[0]
    cos, s1, s2 = cos_ref[...], s1_ref[...], s2_ref[...]

    def proj(c0):
        return _dot(xb, wm_ref[:, c0:c0 + C_W])

    def rope_wide(h):
        return jnp.concatenate(
            [_rope(h[:, j * LANES:(j + 1) * LANES], cos, s1, s2) for j in range(h.shape[1] // LANES)], axis=1)

    q = rope_wide(proj(O_Q))
    k = rope_wide(proj(O_K))
    v = proj(O_V)
    qi = rope_wide(proj(O_QI))
    tail = _dot(xb, wt_ref[...])
    ki2 = _rope(tail[:, 0:LANES], cos, s1, s2)
    k_ref[...] = k
    v_ref[...] = v
    ki_ref[...] = ki2[:, 0:IDX_DIM]
    wi = tail[:, LANES:2 * LANES]
    if sample:
        q_ref[...] = q
        qi_ref[...] = qi
        wi_ref[...] = wi
    else:
        qb_ref[...] = (q * (ATT_SCALE * LOG2E)).astype(bf16)
        kb_ref[...] = k.astype(bf16)
        low = lax.broadcasted_iota(jnp.int32, (1, LANES), 1) < HEAD_DIM
        for j in range(N_HEADS // 2):
            vj = v[:, j * LANES:(j + 1) * LANES]
            va_ref[:, 2 * j * LANES:(2 * j + 1) * LANES] = jnp.where(low, vj, 1.0).astype(bf16)
            va_ref[:, (2 * j + 1) * LANES:(2 * j + 2) * LANES] = jnp.where(low, 1.0, vj).astype(bf16)
        qit_ref[...] = qi.T.astype(bf16)
        kib_ref[...] = ki2[:, 0:IDX_DIM].astype(bf16)
        wit_ref[...] = wi.T[0:IDX_HEADS]

    p = proj(O_P)
    if sample:
        p_ref[...] = p
        n = DEC_BATCH
        hist = hist_ref[...]

        def slab(j, gs):
            if j < POOL_HIST:
                return hist[j * n:(j + 1) * n, gs]
            return p[(j - POOL_HIST) * n:(j - POOL_HIST + 1) * n, gs]

        for g, w in enumerate(POOL_WINDOWS):
            gs = slice(g * C_GROUP_W, (g + 1) * C_GROUP_W)
            for t in range(DEC_SEQ):
                win = slab(POOL_HIST + t, gs)
                for i in range(1, w):
                    win = win + slab(POOL_HIST + t - i, gs)
                pooled = win / float(w) - p[t * n:(t + 1) * n, gs]
                co = _dot(pooled.astype(bf16), cwt_ref[g]) * cs_ref[:, gs]
                co_ref[t * n:(t + 1) * n, gs] = co.astype(bf16)
    else:
        hrows = 2 * SUBLANES
        ext = jnp.concatenate([halo_ref[...], p], axis=0)
        pos = pl.program_id(1) * rows + lax.broadcasted_iota(jnp.int32, (rows, 1), 0)
        for g, w in enumerate(POOL_WINDOWS):
            gs = slice(g * C_GROUP_W, (g + 1) * C_GROUP_W)
            s = ext[:, gs]
            step = 1
            while step < w:
                s = s + pltpu.roll(s, step, 0)
                step *= 2
            cnt = jnp.minimum(w, pos + 1).astype(f32)
            pooled = s[hrows:] / cnt - p[:, gs]
            co = _dot(pooled.astype(bf16), cwt_ref[g]) * cs_ref[:, gs]
            co_ref[:, gs] = co.astype(bf16)
        halo_ref[...] = p[rows - hrows:]
        ph_ref[...] = p[rows - hrows:]


def _odd_pre_prompt(x, wm, wt, cos, s1, s2, cwt, cs):
    nt = SEQ // TM
    row = lambda c: pl.BlockSpec((None, TM, c), lambda b, t: (b, t, 0))
    col = lambda r: pl.BlockSpec((None, r, TM), lambda b, t: (b, 0, t))
    tab = pl.BlockSpec((TM, LANES), lambda b, t: (t, 0))
    sds = lambda c, dt: jax.ShapeDtypeStruct((BATCH, SEQ, c), dt)
    sdt = lambda r, dt: jax.ShapeDtypeStruct((BATCH, r, SEQ), dt)
    hrows = 2 * SUBLANES
    return pl.pallas_call(
        functools.partial(_odd_pre_kernel, sample=False),
        grid=(BATCH, nt),
        in_specs=[row(D_MODEL), _full(wm.shape), _full(wt.shape), tab, tab, tab, _full(cwt.shape),
                  _full(cs.shape)],
        out_specs=[row(ATT_W), row(ATT_W), row(IDX_DIM), row(ATT_W), row(ATT_W), row(2 * ATT_W),
                   col(IDX_HEADS * IDX_DIM), row(IDX_DIM), col(IDX_HEADS), row(C_W),
                   pl.BlockSpec((None, hrows, C_W), lambda b, t: (b, 0, 0))],
        out_shape=[sds(ATT_W, f32), sds(ATT_W, f32), sds(IDX_DIM, f32), sds(ATT_W, bf16), sds(ATT_W, bf16),
                   sds(2 * ATT_W, bf16), sdt(IDX_HEADS * IDX_DIM, bf16), sds(IDX_DIM, bf16), sdt(IDX_HEADS, f32),
                   sds(C_W, bf16), jax.ShapeDtypeStruct((BATCH, hrows, C_W), f32)],
        scratch_shapes=[pltpu.VMEM((hrows, C_W), f32)],
        compiler_params=pltpu.CompilerParams(dimension_semantics=("arbitrary", "arbitrary"),
                                             vmem_limit_bytes=VMEM_LIMIT),
        name="odd_pre_prompt",
    )(x, wm, wt, cos, s1, s2, cwt, cs)


def _odd_pre_sample(x, wm, wt, cos, s1, s2, cwt, cs, hist):
    args = (x, wm, wt, cos, s1, s2, cwt, cs, hist)
    sds = lambda c, dt: jax.ShapeDtypeStruct((S_ROWS, c), dt)
    outs = [sds(ATT_W, f32), sds(ATT_W, f32), sds(IDX_DIM, f32), sds(ATT_W, f32), sds(ATT_W, f32),
            sds(LANES, f32), sds(C_W, bf16), sds(C_W, f32)]
    return pl.pallas_call(
        functools.partial(_odd_pre_kernel, sample=True),
        grid=(1,),
        in_specs=[_full(a.shape) for a in args],
        out_specs=[_full(o.shape) for o in outs],
        out_shape=outs,
        compiler_params=pltpu.CompilerParams(dimension_semantics=("arbitrary",),
                                             vmem_limit_bytes=VMEM_LIMIT),
        name="odd_pre_sample",
    )(*args)


def _order_key(s):
    s = jnp.where(s == 0.0, 0.0, s)
    bits = lax.bitcast_convert_type(s, jnp.int32)
    return bits ^ ((bits >> 31) & 0x7FFFFFFF)


def _dsa_prompt_kernel(q_ref, qit_ref, wit_ref, k_ref, va_ref, ki_ref, o_ref,
                       keys_ref, tpos_ref, m_ref, acc_ref, lga_ref, lgb_ref, da_ref, db_ref):
    qb = pl.program_id(1)
    n_chunks = qb // (KEY_CHUNK // Q_BLOCK) + 1
    qpos = qb * Q_BLOCK + lax.broadcasted_iota(jnp.int32, (1, Q_BLOCK), 1)
    krow = lax.broadcasted_iota(jnp.int32, (KEY_CHUNK, 1), 0)
    kf = float(TOPK)

    qit = qit_ref[...]
    stat = jnp.concatenate([qit[h * IDX_DIM:(h + 1) * IDX_DIM, :] for h in range(IDX_HEADS)], axis=1)
    wit = wit_ref[...]

    def idx_dots(c, d_ref):
        ks = pl.multiple_of(c * KEY_CHUNK, KEY_CHUNK)
        d_ref[...] = _dot(ki_ref[pl.ds(ks, KEY_CHUNK), :], stat)

    def idx_keys(c, d_ref):
        ks = pl.multiple_of(c * KEY_CHUNK, KEY_CHUNK)
        s = jnp.zeros((KEY_CHUNK, Q_BLOCK), f32)
        for h in range(IDX_HEADS):
            s = s + wit[h:h + 1, :] * jnp.maximum(d_ref[:, h * Q_BLOCK:(h + 1) * Q_BLOCK], 0.0)
        keys_ref[pl.ds(ks, KEY_CHUNK), :] = jnp.where((ks + krow) <= qpos, _order_key(s), INT_MIN)

    def two_chunk_pipeline(produce, consume, buf_a, buf_b):
        last = n_chunks - 1
        produce(0, buf_a)

        def body(i, carry):
            produce(2 * i + 1, buf_b)
            consume(2 * i, buf_a)
            produce(jnp.minimum(2 * i + 2, last), buf_a)
            consume(2 * i + 1, buf_b)
            return carry

        lax.fori_loop(0, n_chunks // 2, body, 0)

        @pl.when(lax.rem(n_chunks, 2) == 1)
        def _():
            consume(last, buf_a)

    two_chunk_pipeline(idx_dots, idx_keys, da_ref, db_ref)

    def count(pred, trips, width):
        lanes_acc = 4

        def body(c, accs):
            ks = c * width if isinstance(c, int) else pl.multiple_of(c * width, width)
            m3 = jnp.where(pred(ks, width), 1.0, 0.0).reshape(width // SUBLANES, SUBLANES, LANES)
            accs = list(accs)
            for i in range(width // SUBLANES):
                accs[i % lanes_acc] = accs[i % lanes_acc] + m3[i]
            return tuple(accs)

        accs = (jnp.zeros((SUBLANES, LANES), f32),) * lanes_acc
        if isinstance(trips, int):
            for c in range(trips):
                accs = body(c, accs)
        else:
            accs = lax.fori_loop(0, trips, body, accs)
        return jnp.sum((accs[0] + accs[1]) + (accs[2] + accs[3]), axis=0, keepdims=True)

    def select(n_scan):
        def cnt_ge(cand):
            return count(lambda ks, w: keys_ref[pl.ds(ks, w), :] >= cand, n_scan, SCAN_CHUNK)

        zero = jnp.zeros((1, LANES), jnp.int32)
        c0 = cnt_ge(zero)
        ok0 = c0 >= kf
        t0 = jnp.where(ok0, zero, jnp.full((1, LANES), INT_MIN, jnp.int32))
        ct0 = jnp.where(ok0, c0, jnp.full((1, LANES), 1e9, f32))

        def body(i, st):
            t, ct = st
            cand = t | lax.shift_left(jnp.int32(1), 30 - i)
            c = cnt_ge(cand)
            ok = c >= kf
            return jnp.where(ok, cand, t), jnp.where(ok, c, ct)

        thr, cthr = lax.fori_loop(0, 31, body, (t0, ct0))
        settled = jnp.logical_or(cthr == kf, thr == INT_MIN)

        def tie_search():
            need = kf - count(lambda ks, w: keys_ref[pl.ds(ks, w), :] > thr, n_scan, SCAN_CHUNK)
            far = jnp.int32(2 * SEQ)
            for c in range(n_scan):
                rs = pl.ds(c * SCAN_CHUNK, SCAN_CHUNK)
                pos = c * SCAN_CHUNK + lax.broadcasted_iota(jnp.int32, (SCAN_CHUNK, LANES), 0)
                tpos_ref[rs, :] = jnp.where(keys_ref[rs, :] == thr, pos, far)
            bits = (n_scan * SCAN_CHUNK - 1).bit_length() + 1

            def jbit(i, j):
                cand = j | lax.shift_left(jnp.int32(1), bits - 1 - i)
                c = count(lambda ks, w: tpos_ref[pl.ds(ks, w), :] < cand, n_scan, SCAN_CHUNK)
                return jnp.where(c <= need, cand, j)

            return lax.fori_loop(0, bits, jbit, zero)

        has_tie = jnp.max(jnp.where(settled, 0.0, 1.0)) > 0.0
        jsel = lax.cond(has_tie, tie_search, lambda: jnp.full((1, LANES), 2 * SEQ, jnp.int32))
        return thr, jsel

    n_scan = (qb * Q_BLOCK + Q_BLOCK + SCAN_CHUNK - 1) // SCAN_CHUNK
    thr, jsel = lax.switch(
        n_scan - 1, [functools.partial(select, k) for k in range(1, SEQ // SCAN_CHUNK + 1)])

    low = lax.broadcasted_iota(jnp.int32, (1, LANES), 1) < HEAD_DIM
    pairs = N_HEADS // 2
    m_ref[...] = jnp.full(m_ref.shape, NEG_BIG, f32)
    acc_ref[...] = jnp.zeros_like(acc_ref)
    q = q_ref[...]
    qpair = []
    for j in range(pairs):
        blk = q[:, j * LANES:(j + 1) * LANES]
        nil = jnp.zeros_like(blk)
        qpair.append(jnp.concatenate([jnp.where(low, blk, nil), jnp.where(low, nil, blk)], axis=0))

    def qk_all(c, lg_ref):
        ks = pl.multiple_of(c * KEY_CHUNK, KEY_CHUNK)
        for j in range(pairs):
            lg_ref[j] = _dot_nt(qpair[j], k_ref[pl.ds(ks, KEY_CHUNK), j * LANES:(j + 1) * LANES])

    def soft_pv(c, lg_ref):
        ks = pl.multiple_of(c * KEY_CHUNK, KEY_CHUNK)
        kk = keys_ref[pl.ds(ks, KEY_CHUNK), :]
        pos = ks + lax.broadcasted_iota(jnp.int32, (KEY_CHUNK, LANES), 0)
        b = jnp.where(kk > thr, 0.0, jnp.where(kk == thr, jnp.where(pos < jsel, 0.0, NEG_BIG), NEG_BIG))
        bias = jnp.where(kk == INT_MIN, NEG_BIG, b).T
        bias2 = jnp.concatenate([bias, bias], axis=0)
        for j in range(pairs):
            lg = lg_ref[j] + bias2
            m_old = m_ref[j]
            m_new = jnp.maximum(m_old, jnp.max(lg, axis=1, keepdims=True))
            p = jnp.exp2(lg - m_new).astype(bf16)
            a = jnp.exp2(m_old - m_new)
            pv = _dot(p, va_ref[pl.ds(ks, KEY_CHUNK), 2 * j * LANES:(2 * j + 2) * LANES])
            own = jnp.concatenate([pv[0:Q_BLOCK, 0:LANES], pv[Q_BLOCK:2 * Q_BLOCK, LANES:2 * LANES]], axis=0)
            acc_ref[j] = a * acc_ref[j] + own
            m_ref[j] = m_new

    two_chunk_pipeline(qk_all, soft_pv, lga_ref, lgb_ref)

    for j in range(pairs):
        acc = acc_ref[j]
        out = acc / pltpu.roll(acc, HEAD_DIM, 1)
        o_ref[:, j * LANES:(j + 1) * LANES] = jnp.where(low, out[0:Q_BLOCK], out[Q_BLOCK:2 * Q_BLOCK]).astype(bf16)


def _dsa_prompt(qb, qit, wit, kb, va, kib):
    nq = SEQ // Q_BLOCK
    qrow = lambda c: pl.BlockSpec((None, Q_BLOCK, c), lambda b, t: (b, t, 0))
    qcol = lambda r: pl.BlockSpec((None, r, Q_BLOCK), lambda b, t: (b, 0, t))
    seq = lambda c: pl.BlockSpec((None, SEQ, c), lambda b, t: (b, 0, 0))
    return pl.pallas_call(
        _dsa_prompt_kernel,
        grid=(BATCH, nq),
        in_specs=[qrow(ATT_W), qcol(IDX_HEADS * IDX_DIM), qcol(IDX_HEADS), seq(ATT_W), seq(2 * ATT_W),
                  seq(IDX_DIM)],
        out_specs=qrow(ATT_W),
        out_shape=jax.ShapeDtypeStruct((BATCH, SEQ, ATT_W), bf16),
        scratch_shapes=[pltpu.VMEM((SEQ, Q_BLOCK), jnp.int32),
                        pltpu.VMEM((SEQ, Q_BLOCK), jnp.int32),
                        pltpu.VMEM((N_HEADS // 2, 2 * Q_BLOCK, 1), f32),
                        pltpu.VMEM((N_HEADS // 2, 2 * Q_BLOCK, LANES), f32),
                        pltpu.VMEM((N_HEADS // 2, 2 * Q_BLOCK, KEY_CHUNK), f32),
                        pltpu.VMEM((N_HEADS // 2, 2 * Q_BLOCK, KEY_CHUNK), f32),
                        pltpu.VMEM((KEY_CHUNK, IDX_HEADS * Q_BLOCK), f32),
                        pltpu.VMEM((KEY_CHUNK, IDX_HEADS * Q_BLOCK), f32)],
        compiler_params=pltpu.CompilerParams(dimension_semantics=("arbitrary", "arbitrary"),
                                             vmem_limit_bytes=VMEM_LIMIT),
        name="dsa_prompt",
    )(qb, qit, wit, kb, va, kib)


S_KEYS = PAST_LEN + PAGE_SIZE
BITS_PER_PASS = 4
SEQ_PER_STEP = 2


def _dsa_sample_kernel(pt_ref, q_ref, qi_ref, wi_ref, kin_ref, kn_ref, vn_ref, *rest):
    g_n = SEQ_PER_STEP
    kidx_refs = rest[0:g_n * N_PAGES]
    k_refs = rest[g_n * N_PAGES:2 * g_n * N_PAGES]
    v_refs = rest[2 * g_n * N_PAGES:3 * g_n * N_PAGES]
    o_ref = rest[3 * g_n * N_PAGES]
    del pt_ref
    rows = DEC_SEQ * IDX_HEADS
    kf = float(TOPK)
    kpos = lax.broadcasted_iota(jnp.int32, (1, S_KEYS), 1)
    tq = lax.broadcasted_iota(jnp.int32, (DEC_SEQ, 1), 0)

    def pad_keys(x):
        return jnp.concatenate([x, jnp.zeros((PAGE_SIZE - x.shape[0], x.shape[1]), x.dtype)], axis=0)

    def count(pred):
        return jnp.sum(jnp.where(pred, 1.0, 0.0), axis=1, keepdims=True)

    def order_keys(g):
        qi = qi_ref[g].astype(bf16)
        wi = wi_ref[g]

        def idx_score(d):
            r = jnp.maximum(d, 0.0) * wi
            return jnp.sum(r.reshape(DEC_SEQ, IDX_HEADS, PAGE_SIZE), axis=1)

        parts = [idx_score(_dot(qi, kidx_refs[g * N_PAGES + j][...].astype(bf16))) for j in range(N_PAGES)]
        parts.append(idx_score(_dot_nt(qi, pad_keys(kin_ref[g]).astype(bf16))))
        s = jnp.concatenate(parts, axis=1)
        return jnp.where((kpos - PAST_LEN) <= tq, _order_key(s), INT_MIN)

    hrow = lax.broadcasted_iota(jnp.int32, (rows, ATT_W), 0) % N_HEADS
    hcol = lax.broadcasted_iota(jnp.int32, (rows, ATT_W), 1) // HEAD_DIM
    own = hrow == hcol

    def raw_logits(g):
        q4 = q_ref[g] * ATT_SCALE
        q32 = jnp.broadcast_to(q4[:, None, :], (DEC_SEQ, N_HEADS, ATT_W)).reshape(rows, ATT_W)
        qbd = jnp.where(own, q32, 0.0).astype(bf16)
        kn = pad_keys(kn_ref[g]).astype(bf16)
        return jnp.concatenate([_dot(qbd, k_refs[g * N_PAGES + j][...].astype(bf16)) for j in range(N_PAGES)]
                               + [_dot_nt(qbd, kn)], axis=1)

    lg_raw = [raw_logits(g) for g in range(g_n)]

    keys = jnp.concatenate([order_keys(g) for g in range(g_n)], axis=0)
    sel_rows = g_n * DEC_SEQ
    zero = jnp.zeros((sel_rows, 1), jnp.int32)
    c0 = count(keys >= zero)
    thr = jnp.where(c0 >= kf, zero, jnp.full((sel_rows, 1), INT_MIN, jnp.int32))
    cthr = jnp.where(c0 >= kf, c0, 1e9)
    shift = 31
    while shift > 0:
        nb = min(BITS_PER_PASS, shift)
        shift -= nb
        num = zero
        for c in range(1, 2 ** nb):
            cnt = count(keys >= (thr | (c << shift)))
            ok = cnt >= kf
            num = num + ok.astype(jnp.int32)
            cthr = jnp.where(ok, cnt, cthr)
        thr = thr | lax.shift_left(num, shift)
    need = kf - count(keys > thr)
    settled = jnp.logical_or(cthr == kf, thr == INT_MIN)

    def tie_search():
        def jbit(i, j):
            cand = j | lax.shift_left(jnp.int32(1), 12 - i)
            c = count(jnp.logical_and(keys == thr, kpos < cand))
            return jnp.where(c <= need, cand, j)
        return lax.fori_loop(0, 13, jbit, zero)

    jsel = lax.cond(jnp.max(jnp.where(settled, 0.0, 1.0)) > 0.0, tie_search,
                    lambda: jnp.full((sel_rows, 1), 2 * S_KEYS, jnp.int32))
    sel_all = jnp.logical_and(
        jnp.logical_or(keys > thr, jnp.logical_and(keys == thr, kpos < jsel)), keys != INT_MIN)
    bias_all = jnp.where(sel_all, 0.0, NEG_BIG)

    for g in range(g_n):
        bias4 = bias_all[g * DEC_SEQ:(g + 1) * DEC_SEQ]
        bias = jnp.broadcast_to(bias4[:, None, :], (DEC_SEQ, N_HEADS, S_KEYS)).reshape(rows, S_KEYS)
        lg = lg_raw[g] + bias
        m = jnp.max(lg, axis=1, keepdims=True)
        p = jnp.exp(lg - m)
        l = jnp.sum(p, axis=1, keepdims=True)
        pb = p.astype(bf16)
        pv = _dot(pb[:, PAST_LEN:], pad_keys(vn_ref[g]).astype(bf16))
        for j in range(N_PAGES):
            pv = pv + _dot_nt(pb[:, j * PAGE_SIZE:(j + 1) * PAGE_SIZE], v_refs[g * N_PAGES + j][...].astype(bf16))
        out = jnp.where(own, pv, 0.0) / l
        o_ref[g] = jnp.sum(out.reshape(DEC_SEQ, N_HEADS, ATT_W), axis=1)


def _dsa_sample(page_table, q, qi, wi, ki_new, k_new, v_new, kidx_pages, k_pages, v_pages, layer):
    n_pool = kidx_pages.shape[0] // (DEPTH // 2)
    rows = DEC_SEQ * IDX_HEADS
    g_n = SEQ_PER_STEP
    per_seq = lambda r, c: pl.BlockSpec((g_n, r, c), lambda s, pt: (s, 0, 0))

    def page(r, g, j):
        return pl.BlockSpec((None, r, PAGE_SIZE), lambda s, pt: (layer * n_pool + pt[s * g_n + g, j], 0, 0))

    pages = lambda r: [page(r, g, j) for g in range(g_n) for j in range(N_PAGES)]
    in_specs = [per_seq(DEC_SEQ, ATT_W), per_seq(rows, IDX_DIM), per_seq(rows, 1),
                per_seq(SUBLANES, IDX_DIM), per_seq(SUBLANES, ATT_W), per_seq(SUBLANES, ATT_W)]
    in_specs += pages(IDX_DIM) + pages(ATT_W) + pages(ATT_W)
    grid_spec = pltpu.PrefetchScalarGridSpec(
        num_scalar_prefetch=1, grid=(DEC_BATCH // g_n,), in_specs=in_specs,
        out_specs=per_seq(DEC_SEQ, ATT_W))
    n_in = g_n * N_PAGES
    return pl.pallas_call(
        _dsa_sample_kernel,
        grid_spec=grid_spec,
        out_shape=jax.ShapeDtypeStruct((DEC_BATCH, DEC_SEQ, ATT_W), f32),
        compiler_params=pltpu.CompilerParams(dimension_semantics=("arbitrary",),
                                             vmem_limit_bytes=VMEM_LIMIT),
        name="dsa_sample",
    )(page_table, q, qi, wi, ki_new, k_new, v_new,
      *([kidx_pages] * n_in), *([k_pages] * n_in), *([v_pages] * n_in))


def _rope_tables(pos):
    half = ROT // 2
    inv = np.power(np.float32(ROPE_THETA), -np.arange(half, dtype=np.float32) * np.float32(2.0) / np.float32(ROT))
    ang = np.asarray(pos, np.float32)[:, None] * inv.astype(np.float32)[None, :]
    cos, sin = np.cos(ang).astype(np.float32), np.sin(ang).astype(np.float32)
    t = len(pos)
    rest = HEAD_DIM - ROT
    c64 = np.concatenate([cos, cos, np.ones((t, rest), np.float32)], axis=1)
    s1 = np.concatenate([np.zeros((t, half), np.float32), sin, np.zeros((t, rest), np.float32)], axis=1)
    s2 = np.concatenate([-sin, np.zeros((t, half + rest), np.float32)], axis=1)
    two = lambda a: jnp.asarray(np.concatenate([a, a], axis=1))
    return two(c64), two(s1), two(s2)


def _to_tm(a):
    return a.transpose(1, 0, 2).reshape(a.shape[1] * a.shape[0], a.shape[2])


def _from_tm(a, t=DEC_SEQ):
    return a.reshape(t, DEC_BATCH, a.shape[-1]).transpose(1, 0, 2)


def kernel(x_prompt, x_sample, state_b_conv, state_c_pool, cache_k, cache_v, cache_kidx, state_ffn_conv, page_table, w_in_even, a_ln_g, a_ln_b, a_ws, a_bs, b_conv_w, w_out_even, w_in_odd, c_w, c_scale, w_out_odd, ln_mix_g, ln_mix_b, ffn_w_up, ffn_conv_w, ffn_conv_b, ffn_w_down, ln_ffn_g, ln_ffn_b):
    row = lambda a: a.reshape(1, -1)
    xp = x_prompt
    xs = _to_tm(x_sample)
    outs = {}
    ffn_p, ffn_s = [], []

    rows3 = lambda a: a[:, None, :]
    ffn_wts = (rows3(ln_mix_g), rows3(ln_mix_b), ffn_w_up.astype(bf16), ffn_conv_w, rows3(ffn_conv_b),
               ffn_w_down.astype(bf16), rows3(ln_ffn_g), rows3(ln_ffn_b))

    def post(layer, xp, xs, cap, cbp, cas, cbs, wo):
        wob = wo.astype(bf16)
        xp, st_p = _post_prompt(layer, xp, cap, cbp, wob, *ffn_wts)
        xs, st_s = _post_sample(layer, xs, cas, cbs, wob, *ffn_wts, _to_tm(state_ffn_conv[layer]))
        ffn_p.append(st_p[:, SUBLANES - (CONV_W - 1):])
        ffn_s.append(_from_tm(st_s, CONV_W - 1))
        return xp, xs

    for layer in range(DEPTH):
        if layer % 2 == 0:
            e = layer // 2
            w = w_in_even[e].astype(bf16)
            lg, lb = row(a_ln_g[e]), row(a_ln_b[e])
            bias = jnp.repeat(a_bs[e].T, CHUNK, axis=1)
            wcoef = jnp.repeat(a_ws[e][:, :DEC_SEQ, :DEC_SEQ].transpose(1, 2, 0).reshape(DEC_SEQ * DEC_SEQ, A_GROUPS),
                               CHUNK, axis=1)
            a_p, b_p, bst_p = _even_pre_prompt(xp, w, lg, lb, a_ws[e], bias, b_conv_w[e])
            a_s, b_s, v_s, e_s = _even_pre_sample(xs, w, lg, lb, wcoef, bias, b_conv_w[e],
                                                  _to_tm(state_b_conv[e]))
            outs.setdefault("a_v_s", []).append(_from_tm(v_s))
            outs.setdefault("b_p", []).append(bst_p[:, SUBLANES - (CONV_W - 1):])
            outs.setdefault("b_s", []).append(_from_tm(e_s)[:, DEC_SEQ - (CONV_W - 1):])
            xp, xs = post(layer, xp, xs, a_p, b_p, a_s, b_s, w_out_even[e])
        else:
            o = layer // 2
            wm = w_in_odd[o][:, :O_KI].astype(bf16)
            wki = w_in_odd[o][:, O_KI:O_WI]
            wwi = w_in_odd[o][:, O_WI:]
            wt = jnp.concatenate([wki, wki, wwi, jnp.zeros((D_MODEL, LANES - IDX_HEADS), f32)],
                                 axis=1).astype(bf16)
            cwt = c_w[o].astype(bf16)
            cs = row(c_scale[o])
            tabs_p = _rope_tables(np.arange(SEQ))
            tabs_s = _rope_tables(PAST_LEN + np.repeat(np.arange(DEC_SEQ), DEC_BATCH))
            (k_p, v_p, ki_p, qb, kb, va, qit, kib, wit, co_p, ph_p) = _odd_pre_prompt(
                xp, wm, wt, *tabs_p, cwt, cs)
            att_p = _dsa_prompt(qb, qit, wit, kb, va, kib)
            (k_s, v_s2, ki_s, q_s, qi_s, wi_s, co_s, p_s) = _odd_pre_sample(
                xs, wm, wt, *tabs_s, cwt, cs, _to_tm(state_c_pool[o]))
            rows = DEC_SEQ * IDX_HEADS
            pad8 = lambda a: jnp.pad(_from_tm(a), ((0, 0), (0, SUBLANES - DEC_SEQ), (0, 0)))
            page_t = lambda c, w: jnp.moveaxis(c, 2, -1).reshape(-1, w, PAGE_SIZE)
            att_s = _dsa_sample(
                page_table,
                _from_tm(q_s),
                _from_tm(qi_s).reshape(DEC_BATCH, rows, IDX_DIM),
                _from_tm(wi_s)[:, :, :IDX_HEADS].reshape(DEC_BATCH, rows, 1),
                pad8(ki_s), pad8(k_s), pad8(v_s2),
                page_t(cache_kidx, IDX_DIM), page_t(cache_k, ATT_W), page_t(cache_v, ATT_W), o)
            heads = lambda a: a.reshape(a.shape[0], a.shape[1], N_HEADS, HEAD_DIM)
            outs.setdefault("c_p", []).append(ph_p[:, 2 * SUBLANES - POOL_HIST:])
            outs.setdefault("c_s", []).append(
                jnp.concatenate([state_c_pool[o][:, DEC_SEQ:], _from_tm(p_s)], axis=1))
            outs.setdefault("k_p", []).append(heads(k_p))
            outs.setdefault("v_p", []).append(heads(v_p))
            outs.setdefault("ki_p", []).append(ki_p)
            outs.setdefault("k_s", []).append(heads(_from_tm(k_s)))
            outs.setdefault("v_s", []).append(heads(_from_tm(v_s2)))
            outs.setdefault("ki_s", []).append(_from_tm(ki_s))
            xp, xs = post(layer, xp, xs, co_p, att_p, co_s, _to_tm(att_s).astype(bf16), w_out_odd[o])

    st = lambda name: jnp.stack(outs[name])
    return (xp, _from_tm(xs), st("a_v_s"), st("b_p"), st("b_s"), st("c_p"), st("c_s"),
            st("k_p"), st("v_p"), st("ki_p"), st("k_s"), st("v_s"), st("ki_s"),
            jnp.stack(ffn_p), jnp.stack(ffn_s))
```

```python
import functools

import jax
import jax.numpy as jnp
import numpy as np
from jax import lax
from jax.experimental import pallas as pl
from jax.experimental.pallas import tpu as pltpu

D_MODEL = 1024
BATCH = 4
SEQ = 4096
DEPTH = 2
DEC_BATCH = 128
DEC_SEQ = 4
PAST_LEN = 2048
PAGE_SIZE = 128
N_PAGES = PAST_LEN // PAGE_SIZE

A_W = D_MODEL // 2
B_W = D_MODEL // 2
C_W = D_MODEL // 2
ATT_W = D_MODEL // 2
CHUNK = 128
A_GROUPS = A_W // CHUNK
CONV_W = 3
POOL_WINDOWS = (2, 4, 8, 16)
C_GROUPS = len(POOL_WINDOWS)
C_GROUP_W = C_W // C_GROUPS
POOL_HIST = max(POOL_WINDOWS) - 1
HEAD_DIM = 64
N_HEADS = ATT_W // HEAD_DIM
IDX_HEADS = 8
IDX_DIM = 64
TOPK = 256
Q_BLOCK = 128
ROPE_THETA = 500000.0
ROT = HEAD_DIM // 4
D_FF = ((8 * D_MODEL // 3 + 127) // 128) * 128
ALPHA = (2.0 * DEPTH) ** 0.25
LN_EPS = 1e-5
ATT_SCALE = HEAD_DIM ** -0.5
LOG2E = 1.4426950408889634

E_U, E_V, E_B, E_C, E_H = 0, A_W, 2 * A_W, 2 * A_W + B_W, 2 * A_W + 2 * B_W
O_P, O_Q, O_K, O_V, O_QI = 0, C_W, C_W + ATT_W, C_W + 2 * ATT_W, C_W + 3 * ATT_W
O_KI = O_QI + IDX_HEADS * IDX_DIM
O_WI = O_KI + IDX_DIM

LANES = 128
SUBLANES = 8
VMEM_LIMIT = 56 * 1024 * 1024

TM = 512
TM_POST = 256
FF_CHUNK = 256
KEY_CHUNK = 512
SCAN_CHUNK = 256
S_ROWS = DEC_SEQ * DEC_BATCH
INT_MIN = -2 ** 31
NEG_BIG = -1e30

bf16 = jnp.bfloat16
f32 = jnp.float32


def _ln(x, g, b):
    mu = jnp.mean(x, axis=-1, keepdims=True)
    xc = x - mu
    var = jnp.mean(xc * xc, axis=-1, keepdims=True)
    return xc * lax.rsqrt(var + LN_EPS) * g + b


def _gelu(x):
    c = (2.0 / jnp.pi) ** 0.5
    return x * (0.5 * (1.0 + jnp.tanh(c * (x + 0.044715 * (x * x * x)))))


def _dot(a, b):
    return jnp.dot(a, b, preferred_element_type=f32)


def _dot_nt(a, b):
    return lax.dot_general(a, b, (((1,), (1,)), ((), ())), preferred_element_type=f32)


def _prev_rows_prompt(cur, halo):
    ext = jnp.concatenate([halo, cur], axis=0)
    p1 = pltpu.roll(ext, 1, 0)[SUBLANES:]
    p2 = pltpu.roll(ext, 2, 0)[SUBLANES:]
    return p1, p2


def _prev_rows_sample(cur, hist):
    n = DEC_BATCH
    p1 = jnp.concatenate([hist[n:2 * n], cur[0:3 * n]], axis=0)
    p2 = jnp.concatenate([hist, cur[0:2 * n]], axis=0)
    return p1, p2


def _causal_conv(cur, p1, p2, w_ref, c0, width):
    return (cur * w_ref[2:3, c0:c0 + width] + p1 * w_ref[1:2, c0:c0 + width]
            + p2 * w_ref[0:1, c0:c0 + width])


def _even_pre_kernel(*refs, sample):
    if sample:
        (x_ref, w_ref, lg_ref, lb_ref, wcoef_ref, bias_ref, cw_ref, hist_ref,
         a_ref, b_ref, v_ref, e_ref) = refs
    else:
        (x_ref, w_ref, lg_ref, lb_ref, ws_ref, bias_ref, cw_ref,
         a_ref, b_ref, st_ref, halo_ref) = refs

        @pl.when(pl.program_id(1) == 0)
        def _():
            halo_ref[...] = jnp.zeros_like(halo_ref)

    xb = x_ref[...].astype(bf16)
    rows = xb.shape[0]

    def proj(c0):
        return _dot(xb, w_ref[:, c0:c0 + A_W])

    u = _gelu(proj(E_U))
    v = _ln(_gelu(proj(E_V)), lg_ref[...], lb_ref[...])
    if sample:
        v_ref[...] = v
        n = DEC_BATCH
        for t in range(DEC_SEQ):
            mix = bias_ref[t:t + 1, :]
            for s in range(t + 1):
                r = t * DEC_SEQ + s
                mix = mix + wcoef_ref[r:r + 1, :] * v[s * n:(s + 1) * n]
            a_ref[t * n:(t + 1) * n, :] = (u[t * n:(t + 1) * n] * mix).astype(bf16)
    else:
        vb = v.astype(bf16)
        ri = lax.broadcasted_iota(jnp.int32, (CHUNK, CHUNK), 0)
        ci = lax.broadcasted_iota(jnp.int32, (CHUNK, CHUNK), 1)
        for g in range(A_GROUPS):
            gs = slice(g * CHUNK, (g + 1) * CHUNK)
            wc = jnp.where(ri >= ci, ws_ref[g], 0.0).astype(bf16)
            for c in range(rows // CHUNK):
                rs = slice(c * CHUNK, (c + 1) * CHUNK)
                mix = _dot(wc, vb[rs, gs]) + bias_ref[:, gs]
                a_ref[rs, gs] = (u[rs, gs] * mix).astype(bf16)

    e = proj(E_C) * proj(E_H)
    if sample:
        e_ref[...] = e
        p1, p2 = _prev_rows_sample(e, hist_ref[...])
    else:
        p1, p2 = _prev_rows_prompt(e, halo_ref[...])
        halo_ref[...] = e[rows - SUBLANES:]
        st_ref[...] = e[rows - SUBLANES:]
    conv = _causal_conv(e, p1, p2, cw_ref, 0, B_W)
    b_ref[...] = (proj(E_B) * conv).astype(bf16)


def _full(shape):
    nd = len(shape)
    return pl.BlockSpec(shape, lambda *_: (0,) * nd)


def _even_pre_prompt(x, w, lg, lb, ws, bias, cw):
    nt = SEQ // TM
    row = lambda c: pl.BlockSpec((None, TM, c), lambda b, t: (b, t, 0))
    return pl.pallas_call(
        functools.partial(_even_pre_kernel, sample=False),
        grid=(BATCH, nt),
        in_specs=[row(D_MODEL), _full(w.shape), _full(lg.shape), _full(lb.shape), _full(ws.shape),
                  _full(bias.shape), _full(cw.shape)],
        out_specs=[row(A_W), row(B_W), pl.BlockSpec((None, SUBLANES, B_W), lambda b, t: (b, 0, 0))],
        out_shape=[jax.ShapeDtypeStruct((BATCH, SEQ, A_W), bf16),
                   jax.ShapeDtypeStruct((BATCH, SEQ, B_W), bf16),
                   jax.ShapeDtypeStruct((BATCH, SUBLANES, B_W), f32)],
        scratch_shapes=[pltpu.VMEM((SUBLANES, B_W), f32)],
        compiler_params=pltpu.CompilerParams(dimension_semantics=("arbitrary", "arbitrary"),
                                             vmem_limit_bytes=VMEM_LIMIT),
        name="even_pre_prompt",
    )(x, w, lg, lb, ws, bias, cw)


def _even_pre_sample(x, w, lg, lb, wcoef, bias, cw, hist):
    args = (x, w, lg, lb, wcoef, bias, cw, hist)
    return pl.pallas_call(
        functools.partial(_even_pre_kernel, sample=True),
        grid=(1,),
        in_specs=[_full(a.shape) for a in args],
        out_specs=[_full((S_ROWS, A_W)), _full((S_ROWS, B_W)), _full((S_ROWS, A_W)), _full((S_ROWS, B_W))],
        out_shape=[jax.ShapeDtypeStruct((S_ROWS, A_W), bf16),
                   jax.ShapeDtypeStruct((S_ROWS, B_W), bf16),
                   jax.ShapeDtypeStruct((S_ROWS, A_W), f32),
                   jax.ShapeDtypeStruct((S_ROWS, B_W), f32)],
        compiler_params=pltpu.CompilerParams(dimension_semantics=("arbitrary",),
                                             vmem_limit_bytes=VMEM_LIMIT),
        name="even_pre_sample",
    )(*args)


def _post_kernel(*refs, sample):
    if sample:
        (x_ref, ca_ref, cb_ref, wo_ref, g1_ref, b1_ref, wu_ref, cw_ref, cb2_ref, wd_ref, g2_ref, b2_ref,
         hist_ref, o_ref, st_ref, acc_ref) = refs
    else:
        (x_ref, ca_ref, cb_ref, wo_ref, g1_ref, b1_ref, wu_ref, cw_ref, cb2_ref, wd_ref, g2_ref, b2_ref,
         o_ref, st_ref, acc_ref, halo_ref) = refs

        @pl.when(pl.program_id(1) == 0)
        def _():
            halo_ref[...] = jnp.zeros_like(halo_ref)

    half = ca_ref.shape[-1]
    y = _dot(ca_ref[...], wo_ref[0:half, :]) + _dot(cb_ref[...], wo_ref[half:2 * half, :])
    x1 = _ln(ALPHA * x_ref[...] + y, g1_ref[...], b1_ref[...])
    x1b = x1.astype(bf16)
    rows = x1b.shape[0]

    def up(c0):
        return (_dot(x1b, wu_ref[:, c0:c0 + FF_CHUNK]), _dot(x1b, wu_ref[:, D_FF + c0:D_FF + c0 + FF_CHUNK]))

    def conv_half(hu, c0):
        if sample:
            p1, p2 = _prev_rows_sample(hu, hist_ref[:, c0:c0 + FF_CHUNK])
            st_ref[:, c0:c0 + FF_CHUNK] = hu[2 * DEC_BATCH:]
        else:
            p1, p2 = _prev_rows_prompt(hu, halo_ref[:, c0:c0 + FF_CHUNK])
            halo_ref[:, c0:c0 + FF_CHUNK] = hu[rows - SUBLANES:]
            st_ref[:, c0:c0 + FF_CHUNK] = hu[rows - SUBLANES:]
        return _causal_conv(hu, p1, p2, cw_ref, c0, FF_CHUNK) + cb2_ref[:, c0:c0 + FF_CHUNK]

    n_ff = D_FF // FF_CHUNK
    nxt = up(0)
    for c in range(n_ff):
        c0 = c * FF_CHUNK
        cur = nxt
        if c + 1 < n_ff:
            nxt = up(c0 + FF_CHUNK)
        hg = conv_half(cur[0], c0)
        hv = conv_half(cur[1], D_FF + c0)
        act = (hg * jax.nn.sigmoid(hg) * hv).astype(bf16)
        part = _dot(act, wd_ref[c0:c0 + FF_CHUNK, :])
        if c == 0:
            acc_ref[...] = part
        else:
            acc_ref[...] += part
    o_ref[...] = _ln(ALPHA * x1 + acc_ref[...], g2_ref[...], b2_ref[...])


def _layer_spec(a, layer, **kw):
    return pl.BlockSpec((None,) + a.shape[1:], lambda *_: (layer,) + (0,) * (a.ndim - 1), **kw)


def _post_prompt(layer, x, ca, cb, wo, g1, b1, wu, cw, cb2, wd, g2, b2):
    nt = SEQ // TM_POST
    row = lambda c: pl.BlockSpec((None, TM_POST, c), lambda b, t: (b, t, 0))
    const = lambda a: pl.BlockSpec(a.shape, lambda b, t: (0,) * a.ndim, pipeline_mode=pl.Buffered(1))
    stacked = lambda a: _layer_spec(a, layer, pipeline_mode=pl.Buffered(1))
    return pl.pallas_call(
        functools.partial(_post_kernel, sample=False),
        grid=(BATCH, nt),
        in_specs=[row(D_MODEL), row(ca.shape[-1]), row(cb.shape[-1]), const(wo)]
                 + [stacked(a) for a in (g1, b1, wu, cw, cb2, wd, g2, b2)],
        out_specs=[row(D_MODEL), pl.BlockSpec((None, SUBLANES, 2 * D_FF), lambda b, t: (b, 0, 0))],
        out_shape=[jax.ShapeDtypeStruct((BATCH, SEQ, D_MODEL), f32),
                   jax.ShapeDtypeStruct((BATCH, SUBLANES, 2 * D_FF), f32)],
        scratch_shapes=[pltpu.VMEM((TM_POST, D_MODEL), f32), pltpu.VMEM((SUBLANES, 2 * D_FF), f32)],
        compiler_params=pltpu.CompilerParams(dimension_semantics=("arbitrary", "arbitrary"),
                                             vmem_limit_bytes=VMEM_LIMIT),
        name="post_prompt",
    )(x, ca, cb, wo, g1, b1, wu, cw, cb2, wd, g2, b2)


def _post_sample(layer, x, ca, cb, wo, g1, b1, wu, cw, cb2, wd, g2, b2, hist):
    args = (x, ca, cb, wo, g1, b1, wu, cw, cb2, wd, g2, b2, hist)
    full = lambda a: _full(a.shape)
    stacked = lambda a: _layer_spec(a, layer)
    return pl.pallas_call(
        functools.partial(_post_kernel, sample=True),
        grid=(1,),
        in_specs=[full(x), full(ca), full(cb), full(wo)]
                 + [stacked(a) for a in (g1, b1, wu, cw, cb2, wd, g2, b2)] + [full(hist)],
        out_specs=[_full((S_ROWS, D_MODEL)), _full((2 * DEC_BATCH, 2 * D_FF))],
        out_shape=[jax.ShapeDtypeStruct((S_ROWS, D_MODEL), f32),
                   jax.ShapeDtypeStruct((2 * DEC_BATCH, 2 * D_FF), f32)],
        scratch_shapes=[pltpu.VMEM((S_ROWS, D_MODEL), f32)],
        compiler_params=pltpu.CompilerParams(dimension_semantics=("arbitrary",),
                                             vmem_limit_bytes=VMEM_LIMIT),
        name="post_sample",
    )(*args)


def _rope(blk, cos, s1, s2):
    return blk * cos + pltpu.roll(blk, ROT // 2, 1) * s1 + pltpu.roll(blk, LANES - ROT // 2, 1) * s2


def _odd_pre_kernel(*refs, sample):
    if sample:
        (x_ref, wm_ref, wt_ref, cos_ref, s1_ref, s2_ref, cwt_ref, cs_ref, hist_ref,
         k_ref, v_ref, ki_ref, q_ref, qi_ref, wi_ref, co_ref, p_ref) = refs
    else:
        (x_ref, wm_ref, wt_ref, cos_ref, s1_ref, s2_ref, cwt_ref, cs_ref,
         k_ref, v_ref, ki_ref, qb_ref, kb_ref, va_ref, qit_ref, kib_ref, wit_ref, co_ref, ph_ref,
         halo_ref) = refs

        @pl.when(pl.program_id(1) == 0)
        def _():
            halo_ref[...] = jnp.zeros_like(halo_ref)

    xb = x_ref[...].astype(bf16)
    rows = xb.shape[0]
    cos, s1, s2 = cos_ref[...], s1_ref[...], s2_ref[...]

    def proj(c0):
        return _dot(xb, wm_ref[:, c0:c0 + C_W])

    def rope_wide(h):
        return jnp.concatenate(
            [_rope(h[:, j * LANES:(j + 1) * LANES], cos, s1, s2) for j in range(h.shape[1] // LANES)], axis=1)

    q = rope_wide(proj(O_Q))
    k = rope_wide(proj(O_K))
    v = proj(O_V)
    qi = rope_wide(proj(O_QI))
    tail = _dot(xb, wt_ref[...])
    ki2 = _rope(tail[:, 0:LANES], cos, s1, s2)
    k_ref[...] = k
    v_ref[...] = v
    ki_ref[...] = ki2[:, 0:IDX_DIM]
    wi = tail[:, LANES:2 * LANES]
    if sample:
        q_ref[...] = q
        qi_ref[...] = qi
        wi_ref[...] = wi
    else:
        qb_ref[...] = (q * (ATT_SCALE * LOG2E)).astype(bf16)
        kb_ref[...] = k.astype(bf16)
        low = lax.broadcasted_iota(jnp.int32, (1, LANES), 1) < HEAD_DIM
        for j in range(N_HEADS // 2):
            vj = v[:, j * LANES:(j + 1) * LANES]
            va_ref[:, 2 * j * LANES:(2 * j + 1) * LANES] = jnp.where(low, vj, 1.0).astype(bf16)
            va_ref[:, (2 * j + 1) * LANES:(2 * j + 2) * LANES] = jnp.where(low, 1.0, vj).astype(bf16)
        qit_ref[...] = qi.T.astype(bf16)
        kib_ref[...] = ki2[:, 0:IDX_DIM].astype(bf16)
        wit_ref[...] = wi.T[0:IDX_HEADS]

    p = proj(O_P)
    if sample:
        p_ref[...] = p
        n = DEC_BATCH
        hist = hist_ref[...]

        def slab(j, gs):
            if j < POOL_HIST:
                return hist[j * n:(j + 1) * n, gs]
            return p[(j - POOL_HIST) * n:(j - POOL_HIST + 1) * n, gs]

        for g, w in enumerate(POOL_WINDOWS):
            gs = slice(g * C_GROUP_W, (g + 1) * C_GROUP_W)
            for t in range(DEC_SEQ):
                win = slab(POOL_HIST + t, gs)
                for i in range(1, w):
                    win = win + slab(POOL_HIST + t - i, gs)
                pooled = win / float(w) - p[t * n:(t + 1) * n, gs]
                co = _dot(pooled.astype(bf16), cwt_ref[g]) * cs_ref[:, gs]
                co_ref[t * n:(t + 1) * n, gs] = co.astype(bf16)
    else:
        hrows = 2 * SUBLANES
        ext = jnp.concatenate([halo_ref[...], p], axis=0)
        pos = pl.program_id(1) * rows + lax.broadcasted_iota(jnp.int32, (rows, 1), 0)
        for g, w in enumerate(POOL_WINDOWS):
            gs = slice(g * C_GROUP_W, (g + 1) * C_GROUP_W)
            s = ext[:, gs]
            step = 1
            while step < w:
                s = s + pltpu.roll(s, step, 0)
                step *= 2
            cnt = jnp.minimum(w, pos + 1).astype(f32)
            pooled = s[hrows:] / cnt - p[:, gs]
            co = _dot(pooled.astype(bf16), cwt_ref[g]) * cs_ref[:, gs]
            co_ref[:, gs] = co.astype(bf16)
        halo_ref[...] = p[rows - hrows:]
        ph_ref[...] = p[rows - hrows:]


def _odd_pre_prompt(x, wm, wt, cos, s1, s2, cwt, cs):
    nt = SEQ // TM
    row = lambda c: pl.BlockSpec((None, TM, c), lambda b, t: (b, t, 0))
    col = lambda r: pl.BlockSpec((None, r, TM), lambda b, t: (b, 0, t))
    tab = pl.BlockSpec((TM, LANES), lambda b, t: (t, 0))
    sds = lambda c, dt: jax.ShapeDtypeStruct((BATCH, SEQ, c), dt)
    sdt = lambda r, dt: jax.ShapeDtypeStruct((BATCH, r, SEQ), dt)
    hrows = 2 * SUBLANES
    return pl.pallas_call(
        functools.partial(_odd_pre_kernel, sample=False),
        grid=(BATCH, nt),
        in_specs=[row(D_MODEL), _full(wm.shape), _full(wt.shape), tab, tab, tab, _full(cwt.shape),
                  _full(cs.shape)],
        out_specs=[row(ATT_W), row(ATT_W), row(IDX_DIM), row(ATT_W), row(ATT_W), row(2 * ATT_W),
                   col(IDX_HEADS * IDX_DIM), row(IDX_DIM), col(IDX_HEADS), row(C_W),
                   pl.BlockSpec((None, hrows, C_W), lambda b, t: (b, 0, 0))],
        out_shape=[sds(ATT_W, f32), sds(ATT_W, f32), sds(IDX_DIM, f32), sds(ATT_W, bf16), sds(ATT_W, bf16),
                   sds(2 * ATT_W, bf16), sdt(IDX_HEADS * IDX_DIM, bf16), sds(IDX_DIM, bf16), sdt(IDX_HEADS, f32),
                   sds(C_W, bf16), jax.ShapeDtypeStruct((BATCH, hrows, C_W), f32)],
        scratch_shapes=[pltpu.VMEM((hrows, C_W), f32)],
        compiler_params=pltpu.CompilerParams(dimension_semantics=("arbitrary", "arbitrary"),
                                             vmem_limit_bytes=VMEM_LIMIT),
        name="odd_pre_prompt",
    )(x, wm, wt, cos, s1, s2, cwt, cs)


def _odd_pre_sample(x, wm, wt, cos, s1, s2, cwt, cs, hist):
    args = (x, wm, wt, cos, s1, s2, cwt, cs, hist)
    sds = lambda c, dt: jax.ShapeDtypeStruct((S_ROWS, c), dt)
    outs = [sds(ATT_W, f32), sds(ATT_W, f32), sds(IDX_DIM, f32), sds(ATT_W, f32), sds(ATT_W, f32),
            sds(LANES, f32), sds(C_W, bf16), sds(C_W, f32)]
    return pl.pallas_call(
        functools.partial(_odd_pre_kernel, sample=True),
        grid=(1,),
        in_specs=[_full(a.shape) for a in args],
        out_specs=[_full(o.shape) for o in outs],
        out_shape=outs,
        compiler_params=pltpu.CompilerParams(dimension_semantics=("arbitrary",),
                                             vmem_limit_bytes=VMEM_LIMIT),
        name="odd_pre_sample",
    )(*args)


def _order_key(s):
    s = jnp.where(s == 0.0, 0.0, s)
    bits = lax.bitcast_convert_type(s, jnp.int32)
    return bits ^ ((bits >> 31) & 0x7FFFFFFF)


def _dsa_prompt_kernel(q_ref, qit_ref, wit_ref, k_ref, va_ref, ki_ref, o_ref,
                       keys_ref, tpos_ref, m_ref, acc_ref, lga_ref, lgb_ref, da_ref, db_ref):
    qb = pl.program_id(1)
    n_chunks = qb // (KEY_CHUNK // Q_BLOCK) + 1
    qpos = qb * Q_BLOCK + lax.broadcasted_iota(jnp.int32, (1, Q_BLOCK), 1)
    krow = lax.broadcasted_iota(jnp.int32, (KEY_CHUNK, 1), 0)
    kf = float(TOPK)

    qit = qit_ref[...]
    stat = jnp.concatenate([qit[h * IDX_DIM:(h + 1) * IDX_DIM, :] for h in range(IDX_HEADS)], axis=1)
    wit = wit_ref[...]

    def idx_dots(c, d_ref):
        ks = pl.multiple_of(c * KEY_CHUNK, KEY_CHUNK)
        d_ref[...] = _dot(ki_ref[pl.ds(ks, KEY_CHUNK), :], stat)

    def idx_keys(c, d_ref):
        ks = pl.multiple_of(c * KEY_CHUNK, KEY_CHUNK)
        s = jnp.zeros((KEY_CHUNK, Q_BLOCK), f32)
        for h in range(IDX_HEADS):
            s = s + wit[h:h + 1, :] * jnp.maximum(d_ref[:, h * Q_BLOCK:(h + 1) * Q_BLOCK], 0.0)
        keys_ref[pl.ds(ks, KEY_CHUNK), :] = jnp.where((ks + krow) <= qpos, _order_key(s), INT_MIN)

    def two_chunk_pipeline(produce, consume, buf_a, buf_b):
        last = n_chunks - 1
        produce(0, buf_a)

        def body(i, carry):
            produce(2 * i + 1, buf_b)
            consume(2 * i, buf_a)
            produce(jnp.minimum(2 * i + 2, last), buf_a)
            consume(2 * i + 1, buf_b)
            return carry

        lax.fori_loop(0, n_chunks // 2, body, 0)

        @pl.when(lax.rem(n_chunks, 2) == 1)
        def _():
            consume(last, buf_a)

    two_chunk_pipeline(idx_dots, idx_keys, da_ref, db_ref)

    def count(pred, trips, width):
        lanes_acc = 4

        def body(c, accs):
            ks = c * width if isinstance(c, int) else pl.multiple_of(c * width, width)
            m3 = jnp.where(pred(ks, width), 1.0, 0.0).reshape(width // SUBLANES, SUBLANES, LANES)
            accs = list(accs)
            for i in range(width // SUBLANES):
                accs[i % lanes_acc] = accs[i % lanes_acc] + m3[i]
            return tuple(accs)

        accs = (jnp.zeros((SUBLANES, LANES), f32),) * lanes_acc
        if isinstance(trips, int):
            for c in range(trips):
                accs = body(c, accs)
        else:
            accs = lax.fori_loop(0, trips, body, accs)
        return jnp.sum((accs[0] + accs[1]) + (accs[2] + accs[3]), axis=0, keepdims=True)

    def select(n_scan):
        def cnt_ge(cand):
            return count(lambda ks, w: keys_ref[pl.ds(ks, w), :] >= cand, n_scan, SCAN_CHUNK)

        zero = jnp.zeros((1, LANES), jnp.int32)
        c0 = cnt_ge(zero)
        ok0 = c0 >= kf
        t0 = jnp.where(ok0, zero, jnp.full((1, LANES), INT_MIN, jnp.int32))
        ct0 = jnp.where(ok0, c0, jnp.full((1, LANES), 1e9, f32))

        def body(i, st):
            t, ct = st
            cand = t | lax.shift_left(jnp.int32(1), 30 - i)
            c = cnt_ge(cand)
            ok = c >= kf
            return jnp.where(ok, cand, t), jnp.where(ok, c, ct)

        thr, cthr = lax.fori_loop(0, 31, body, (t0, ct0))
        settled = jnp.logical_or(cthr == kf, thr == INT_MIN)

        def tie_search():
            need = kf - count(lambda ks, w: keys_ref[pl.ds(ks, w), :] > thr, n_scan, SCAN_CHUNK)
            far = jnp.int32(2 * SEQ)
            for c in range(n_scan):
                rs = pl.ds(c * SCAN_CHUNK, SCAN_CHUNK)
                pos = c * SCAN_CHUNK + lax.broadcasted_iota(jnp.int32, (SCAN_CHUNK, LANES), 0)
                tpos_ref[rs, :] = jnp.where(keys_ref[rs, :] == thr, pos, far)
            bits = (n_scan * SCAN_CHUNK - 1).bit_length() + 1

            def jbit(i, j):
                cand = j | lax.shift_left(jnp.int32(1), bits - 1 - i)
                c = count(lambda ks, w: tpos_ref[pl.ds(ks, w), :] < cand, n_scan, SCAN_CHUNK)
                return jnp.where(c <= need, cand, j)

            return lax.fori_loop(0, bits, jbit, zero)

        has_tie = jnp.max(jnp.where(settled, 0.0, 1.0)) > 0.0
        jsel = lax.cond(has_tie, tie_search, lambda: jnp.full((1, LANES), 2 * SEQ, jnp.int32))
        return thr, jsel

    n_scan = (qb * Q_BLOCK + Q_BLOCK + SCAN_CHUNK - 1) // SCAN_CHUNK
    thr, jsel = lax.switch(
        n_scan - 1, [functools.partial(select, k) for k in range(1, SEQ // SCAN_CHUNK + 1)])

    low = lax.broadcasted_iota(jnp.int32, (1, LANES), 1) < HEAD_DIM
    pairs = N_HEADS // 2
    m_ref[...] = jnp.full(m_ref.shape, NEG_BIG, f32)
    acc_ref[...] = jnp.zeros_like(acc_ref)
    q = q_ref[...]
    eye = (lax.broadcasted_iota(jnp.int32, (Q_BLOCK, Q_BLOCK), 0)
           == lax.broadcasted_iota(jnp.int32, (Q_BLOCK, Q_BLOCK), 1)).astype(bf16)
    qpair = []
    for j in range(pairs):
        blk = q[:, j * LANES:(j + 1) * LANES]
        nil = jnp.zeros_like(blk)
        rows2 = jnp.concatenate([jnp.where(low, blk, nil), jnp.where(low, nil, blk)], axis=0)
        qpair.append(jnp.concatenate([rows2, jnp.concatenate([eye, eye], axis=0)], axis=1))

    def qk_all(c, lg_ref):
        ks = pl.multiple_of(c * KEY_CHUNK, KEY_CHUNK)
        kk = keys_ref[pl.ds(ks, KEY_CHUNK), :]
        pos = ks + lax.broadcasted_iota(jnp.int32, (KEY_CHUNK, LANES), 0)
        b = jnp.where(kk > thr, 0.0, jnp.where(kk == thr, jnp.where(pos < jsel, 0.0, NEG_BIG), NEG_BIG))
        bias_t = jnp.where(kk == INT_MIN, NEG_BIG, b).astype(bf16)
        for j in range(pairs):
            rhs = jnp.concatenate([k_ref[pl.ds(ks, KEY_CHUNK), j * LANES:(j + 1) * LANES], bias_t], axis=1)
            lg_ref[j] = _dot_nt(qpair[j], rhs)

    def soft_pv(c, lg_ref):
        ks = pl.multiple_of(c * KEY_CHUNK, KEY_CHUNK)
        for j in range(pairs):
            lg = lg_ref[j]
            m_old = m_ref[j]
            m_new = jnp.maximum(m_old, jnp.max(lg, axis=1, keepdims=True))
            p = jnp.exp2(lg - m_new).astype(bf16)
            a = jnp.exp2(m_old - m_new)
            pv = _dot(p, va_ref[pl.ds(ks, KEY_CHUNK), 2 * j * LANES:(2 * j + 2) * LANES])
            own = jnp.concatenate([pv[0:Q_BLOCK, 0:LANES], pv[Q_BLOCK:2 * Q_BLOCK, LANES:2 * LANES]], axis=0)
            acc_ref[j] = a * acc_ref[j] + own
            m_ref[j] = m_new

    two_chunk_pipeline(qk_all, soft_pv, lga_ref, lgb_ref)

    for j in range(pairs):
        acc = acc_ref[j]
        out = acc / pltpu.roll(acc, HEAD_DIM, 1)
        o_ref[:, j * LANES:(j + 1) * LANES] = jnp.where(low, out[0:Q_BLOCK], out[Q_BLOCK:2 * Q_BLOCK]).astype(bf16)


def _dsa_prompt(qb, qit, wit, kb, va, kib):
    nq = SEQ // Q_BLOCK
    qrow = lambda c: pl.BlockSpec((None, Q_BLOCK, c), lambda b, t: (b, t, 0))
    qcol = lambda r: pl.BlockSpec((None, r, Q_BLOCK), lambda b, t: (b, 0, t))
    seq = lambda c: pl.BlockSpec((None, SEQ, c), lambda b, t: (b, 0, 0))
    return pl.pallas_call(
        _dsa_prompt_kernel,
        grid=(BATCH, nq),
        in_specs=[qrow(ATT_W), qcol(IDX_HEADS * IDX_DIM), qcol(IDX_HEADS), seq(ATT_W), seq(2 * ATT_W),
                  seq(IDX_DIM)],
        out_specs=qrow(ATT_W),
        out_shape=jax.ShapeDtypeStruct((BATCH, SEQ, ATT_W), bf16),
        scratch_shapes=[pltpu.VMEM((SEQ, Q_BLOCK), jnp.int32),
                        pltpu.VMEM((SEQ, Q_BLOCK), jnp.int32),
                        pltpu.VMEM((N_HEADS // 2, 2 * Q_BLOCK, 1), f32),
                        pltpu.VMEM((N_HEADS // 2, 2 * Q_BLOCK, LANES), f32),
                        pltpu.VMEM((N_HEADS // 2, 2 * Q_BLOCK, KEY_CHUNK), f32),
                        pltpu.VMEM((N_HEADS // 2, 2 * Q_BLOCK, KEY_CHUNK), f32),
                        pltpu.VMEM((KEY_CHUNK, IDX_HEADS * Q_BLOCK), f32),
                        pltpu.VMEM((KEY_CHUNK, IDX_HEADS * Q_BLOCK), f32)],
        compiler_params=pltpu.CompilerParams(dimension_semantics=("arbitrary", "arbitrary"),
                                             vmem_limit_bytes=VMEM_LIMIT),
        name="dsa_prompt",
    )(qb, qit, wit, kb, va, kib)


S_KEYS = PAST_LEN + PAGE_SIZE
BITS_PER_PASS = 4
SEQ_PER_STEP = 2


def _dsa_sample_kernel(pt_ref, q_ref, qi_ref, wi_ref, kin_ref, kn_ref, vn_ref, *rest):
    g_n = SEQ_PER_STEP
    kidx_refs = rest[0:g_n * N_PAGES]
    k_refs = rest[g_n * N_PAGES:2 * g_n * N_PAGES]
    v_refs = rest[2 * g_n * N_PAGES:3 * g_n * N_PAGES]
    o_ref = rest[3 * g_n * N_PAGES]
    del pt_ref
    rows = DEC_SEQ * IDX_HEADS
    kf = float(TOPK)
    kpos = lax.broadcasted_iota(jnp.int32, (1, S_KEYS), 1)
    tq = lax.broadcasted_iota(jnp.int32, (DEC_SEQ, 1), 0)

    def pad_keys(x):
        return jnp.concatenate([x, jnp.zeros((PAGE_SIZE - x.shape[0], x.shape[1]), x.dtype)], axis=0)

    def count(pred):
        return jnp.sum(jnp.where(pred, 1.0, 0.0), axis=1, keepdims=True)

    def order_keys(g):
        qi = qi_ref[g].astype(bf16)
        wi = wi_ref[g]

        def idx_score(d):
            r = jnp.maximum(d, 0.0) * wi
            return jnp.sum(r.reshape(DEC_SEQ, IDX_HEADS, PAGE_SIZE), axis=1)

        parts = [idx_score(_dot(qi, kidx_refs[g * N_PAGES + j][...].astype(bf16))) for j in range(N_PAGES)]
        parts.append(idx_score(_dot_nt(qi, pad_keys(kin_ref[g]).astype(bf16))))
        s = jnp.concatenate(parts, axis=1)
        return jnp.where((kpos - PAST_LEN) <= tq, _order_key(s), INT_MIN)

    hrow = lax.broadcasted_iota(jnp.int32, (rows, ATT_W), 0) % N_HEADS
    hcol = lax.broadcasted_iota(jnp.int32, (rows, ATT_W), 1) // HEAD_DIM
    own = hrow == hcol

    def raw_logits(g):
        q4 = q_ref[g] * ATT_SCALE
        q32 = jnp.broadcast_to(q4[:, None, :], (DEC_SEQ, N_HEADS, ATT_W)).reshape(rows, ATT_W)
        qbd = jnp.where(own, q32, 0.0).astype(bf16)
        kn = pad_keys(kn_ref[g]).astype(bf16)
        return jnp.concatenate([_dot(qbd, k_refs[g * N_PAGES + j][...].astype(bf16)) for j in range(N_PAGES)]
                               + [_dot_nt(qbd, kn)], axis=1)

    lg_raw = [raw_logits(g) for g in range(g_n)]

    keys = jnp.concatenate([order_keys(g) for g in range(g_n)], axis=0)
    sel_rows = g_n * DEC_SEQ
    zero = jnp.zeros((sel_rows, 1), jnp.int32)
    c0 = count(keys >= zero)
    thr = jnp.where(c0 >= kf, zero, jnp.full((sel_rows, 1), INT_MIN, jnp.int32))
    cthr = jnp.where(c0 >= kf, c0, 1e9)
    shift = 31
    while shift > 0:
        nb = min(BITS_PER_PASS, shift)
        shift -= nb
        num = zero
        for c in range(1, 2 ** nb):
            cnt = count(keys >= (thr | (c << shift)))
            ok = cnt >= kf
            num = num + ok.astype(jnp.int32)
            cthr = jnp.where(ok, cnt, cthr)
        thr = thr | lax.shift_left(num, shift)
    need = kf - count(keys > thr)
    settled = jnp.logical_or(cthr == kf, thr == INT_MIN)

    def tie_search():
        def jbit(i, j):
            cand = j | lax.shift_left(jnp.int32(1), 12 - i)
            c = count(jnp.logical_and(keys == thr, kpos < cand))
            return jnp.where(c <= need, cand, j)
        return lax.fori_loop(0, 13, jbit, zero)

    jsel = lax.cond(jnp.max(jnp.where(settled, 0.0, 1.0)) > 0.0, tie_search,
                    lambda: jnp.full((sel_rows, 1), 2 * S_KEYS, jnp.int32))
    sel_all = jnp.logical_and(
        jnp.logical_or(keys > thr, jnp.logical_and(keys == thr, kpos < jsel)), keys != INT_MIN)
    bias_all = jnp.where(sel_all, 0.0, NEG_BIG)

    for g in range(g_n):
        bias4 = bias_all[g * DEC_SEQ:(g + 1) * DEC_SEQ]
        bias = jnp.broadcast_to(bias4[:, None, :], (DEC_SEQ, N_HEADS, S_KEYS)).reshape(rows, S_KEYS)
        lg = lg_raw[g] + bias
        m = jnp.max(lg, axis=1, keepdims=True)
        p = jnp.exp(lg - m)
        l = jnp.sum(p, axis=1, keepdims=True)
        pb = p.astype(bf16)
        pv = _dot(pb[:, PAST_LEN:], pad_keys(vn_ref[g]).astype(bf16))
        for j in range(N_PAGES):
            pv = pv + _dot_nt(pb[:, j * PAGE_SIZE:(j + 1) * PAGE_SIZE], v_refs[g * N_PAGES + j][...].astype(bf16))
        out = jnp.where(own, pv, 0.0) / l
        o_ref[g] = jnp.sum(out.reshape(DEC_SEQ, N_HEADS, ATT_W), axis=1)


def _dsa_sample(page_table, q, qi, wi, ki_new, k_new, v_new, kidx_pages, k_pages, v_pages, layer):
    n_pool = kidx_pages.shape[0] // (DEPTH // 2)
    rows = DEC_SEQ * IDX_HEADS
    g_n = SEQ_PER_STEP
    per_seq = lambda r, c: pl.BlockSpec((g_n, r, c), lambda s, pt: (s, 0, 0))

    def page(r, g, j):
        return pl.BlockSpec((None, r, PAGE_SIZE), lambda s, pt: (layer * n_pool + pt[s * g_n + g, j], 0, 0))

    pages = lambda r: [page(r, g, j) for g in range(g_n) for j in range(N_PAGES)]
    in_specs = [per_seq(DEC_SEQ, ATT_W), per_seq(rows, IDX_DIM), per_seq(rows, 1),
                per_seq(SUBLANES, IDX_DIM), per_seq(SUBLANES, ATT_W), per_seq(SUBLANES, ATT_W)]
    in_specs += pages(IDX_DIM) + pages(ATT_W) + pages(ATT_W)
    grid_spec = pltpu.PrefetchScalarGridSpec(
        num_scalar_prefetch=1, grid=(DEC_BATCH // g_n,), in_specs=in_specs,
        out_specs=per_seq(DEC_SEQ, ATT_W))
    n_in = g_n * N_PAGES
    return pl.pallas_call(
        _dsa_sample_kernel,
        grid_spec=grid_spec,
        out_shape=jax.ShapeDtypeStruct((DEC_BATCH, DEC_SEQ, ATT_W), f32),
        compiler_params=pltpu.CompilerParams(dimension_semantics=("arbitrary",),
                                             vmem_limit_bytes=VMEM_LIMIT),
        name="dsa_sample",
    )(page_table, q, qi, wi, ki_new, k_new, v_new,
      *([kidx_pages] * n_in), *([k_pages] * n_in), *([v_pages] * n_in))


def _rope_tables(pos):
    half = ROT // 2
    inv = np.power(np.float32(ROPE_THETA), -np.arange(half, dtype=np.float32) * np.float32(2.0) / np.float32(ROT))
    ang = np.asarray(pos, np.float32)[:, None] * inv.astype(np.float32)[None, :]
    cos, sin = np.cos(ang).astype(np.float32), np.sin(ang).astype(np.float32)
    t = len(pos)
    rest = HEAD_DIM - ROT
    c64 = np.concatenate([cos, cos, np.ones((t, rest), np.float32)], axis=1)
    s1 = np.concatenate([np.zeros((t, half), np.float32), sin, np.zeros((t, rest), np.float32)], axis=1)
    s2 = np.concatenate([-sin, np.zeros((t, half + rest), np.float32)], axis=1)
    two = lambda a: jnp.asarray(np.concatenate([a, a], axis=1))
    return two(c64), two(s1), two(s2)


def _to_tm(a):
    return a.transpose(1, 0, 2).reshape(a.shape[1] * a.shape[0], a.shape[2])


def _from_tm(a, t=DEC_SEQ):
    return a.reshape(t, DEC_BATCH, a.shape[-1]).transpose(1, 0, 2)


def kernel(x_prompt, x_sample, state_b_conv, state_c_pool, cache_k, cache_v, cache_kidx, state_ffn_conv, page_table, w_in_even, a_ln_g, a_ln_b, a_ws, a_bs, b_conv_w, w_out_even, w_in_odd, c_w, c_scale, w_out_odd, ln_mix_g, ln_mix_b, ffn_w_up, ffn_conv_w, ffn_conv_b, ffn_w_down, ln_ffn_g, ln_ffn_b):
    row = lambda a: a.reshape(1, -1)
    xp = x_prompt
    xs = _to_tm(x_sample)
    outs = {}
    ffn_p, ffn_s = [], []

    rows3 = lambda a: a[:, None, :]
    ffn_wts = (rows3(ln_mix_g), rows3(ln_mix_b), ffn_w_up.astype(bf16), ffn_conv_w, rows3(ffn_conv_b),
               ffn_w_down.astype(bf16), rows3(ln_ffn_g), rows3(ln_ffn_b))

    def post(layer, xp, xs, cap, cbp, cas, cbs, wo):
        wob = wo.astype(bf16)
        xp, st_p = _post_prompt(layer, xp, cap, cbp, wob, *ffn_wts)
        xs, st_s = _post_sample(layer, xs, cas, cbs, wob, *ffn_wts, _to_tm(state_ffn_conv[layer]))
        ffn_p.append(st_p[:, SUBLANES - (CONV_W - 1):])
        ffn_s.append(_from_tm(st_s, CONV_W - 1))
        return xp, xs

    for layer in range(DEPTH):
        if layer % 2 == 0:
            e = layer // 2
            w = w_in_even[e].astype(bf16)
            lg, lb = row(a_ln_g[e]), row(a_ln_b[e])
            bias = jnp.repeat(a_bs[e].T, CHUNK, axis=1)
            wcoef = jnp.repeat(a_ws[e][:, :DEC_SEQ, :DEC_SEQ].transpose(1, 2, 0).reshape(DEC_SEQ * DEC_SEQ, A_GROUPS),
                               CHUNK, axis=1)
            a_p, b_p, bst_p = _even_pre_prompt(xp, w, lg, lb, a_ws[e], bias, b_conv_w[e])
            a_s, b_s, v_s, e_s = _even_pre_sample(xs, w, lg, lb, wcoef, bias, b_conv_w[e],
                                                  _to_tm(state_b_conv[e]))
            outs.setdefault("a_v_s", []).append(_from_tm(v_s))
            outs.setdefault("b_p", []).append(bst_p[:, SUBLANES - (CONV_W - 1):])
            outs.setdefault("b_s", []).append(_from_tm(e_s)[:, DEC_SEQ - (CONV_W - 1):])
            xp, xs = post(layer, xp, xs, a_p, b_p, a_s, b_s, w_out_even[e])
        else:
            o = layer // 2
            wm = w_in_odd[o][:, :O_KI].astype(bf16)
            wki = w_in_odd[o][:, O_KI:O_WI]
            wwi = w_in_odd[o][:, O_WI:]
            wt = jnp.concatenate([wki, wki, wwi, jnp.zeros((D_MODEL, LANES - IDX_HEADS), f32)],
                                 axis=1).astype(bf16)
            cwt = c_w[o].astype(bf16)
            cs = row(c_scale[o])
            tabs_p = _rope_tables(np.arange(SEQ))
            tabs_s = _rope_tables(PAST_LEN + np.repeat(np.arange(DEC_SEQ), DEC_BATCH))
            (k_p, v_p, ki_p, qb, kb, va, qit, kib, wit, co_p, ph_p) = _odd_pre_prompt(
                xp, wm, wt, *tabs_p, cwt, cs)
            att_p = _dsa_prompt(qb, qit, wit, kb, va, kib)
            (k_s, v_s2, ki_s, q_s, qi_s, wi_s, co_s, p_s) = _odd_pre_sample(
                xs, wm, wt, *tabs_s, cwt, cs, _to_tm(state_c_pool[o]))
            rows = DEC_SEQ * IDX_HEADS
            pad8 = lambda a: jnp.pad(_from_tm(a), ((0, 0), (0, SUBLANES - DEC_SEQ), (0, 0)))
            page_t = lambda c, w: jnp.moveaxis(c, 2, -1).reshape(-1, w, PAGE_SIZE)
            att_s = _dsa_sample(
                page_table,
                _from_tm(q_s),
                _from_tm(qi_s).reshape(DEC_BATCH, rows, IDX_DIM),
                _from_tm(wi_s)[:, :, :IDX_HEADS].reshape(DEC_BATCH, rows, 1),
                pad8(ki_s), pad8(k_s), pad8(v_s2),
                page_t(cache_kidx, IDX_DIM), page_t(cache_k, ATT_W), page_t(cache_v, ATT_W), o)
            heads = lambda a: a.reshape(a.shape[0], a.shape[1], N_HEADS, HEAD_DIM)
            outs.setdefault("c_p", []).append(ph_p[:, 2 * SUBLANES - POOL_HIST:])
            outs.setdefault("c_s", []).append(
                jnp.concatenate([state_c_pool[o][:, DEC_SEQ:], _from_tm(p_s)], axis=1))
            outs.setdefault("k_p", []).append(heads(k_p))
            outs.setdefault("v_p", []).append(heads(v_p))
            outs.setdefault("ki_p", []).append(ki_p)
            outs.setdefault("k_s", []).append(heads(_from_tm(k_s)))
            outs.setdefault("v_s", []).append(heads(_from_tm(v_s2)))
            outs.setdefault("ki_s", []).append(_from_tm(ki_s))
            xp, xs = post(layer, xp, xs, co_p, att_p, co_s, _to_tm(att_s).astype(bf16), w_out_odd[o])

    st = lambda name: jnp.stack(outs[name])
    return (xp, _from_tm(xs), st("a_v_s"), st("b_p"), st("b_s"), st("c_p"), st("c_s"),
            st("k_p"), st("v_p"), st("ki_p"), st("k_s"), st("v_s"), st("ki_s"),
            jnp.stack(ffn_p), jnp.stack(ffn_s))
```

```python
import functools

import jax
import jax.numpy as jnp
import numpy as np
from jax import lax
from jax.experimental import pallas as pl
from jax.experimental.pallas import tpu as pltpu

D_MODEL = 1024
BATCH = 4
SEQ = 4096
DEPTH = 2
DEC_BATCH = 128
DEC_SEQ = 4
PAST_LEN = 2048
PAGE_SIZE = 128
N_PAGES = PAST_LEN // PAGE_SIZE

A_W = D_MODEL // 2
B_W = D_MODEL // 2
C_W = D_MODEL // 2
ATT_W = D_MODEL // 2
CHUNK = 128
A_GROUPS = A_W // CHUNK
CONV_W = 3
POOL_WINDOWS = (2, 4, 8, 16)
C_GROUPS = len(POOL_WINDOWS)
C_GROUP_W = C_W // C_GROUPS
POOL_HIST = max(POOL_WINDOWS) - 1
HEAD_DIM = 64
N_HEADS = ATT_W // HEAD_DIM
IDX_HEADS = 8
IDX_DIM = 64
TOPK = 256
Q_BLOCK = 128
ROPE_THETA = 500000.0
ROT = HEAD_DIM // 4
D_FF = ((8 * D_MODEL // 3 + 127) // 128) * 128
ALPHA = (2.0 * DEPTH) ** 0.25
LN_EPS = 1e-5
ATT_SCALE = HEAD_DIM ** -0.5
LOG2E = 1.4426950408889634

E_U, E_V, E_B, E_C, E_H = 0, A_W, 2 * A_W, 2 * A_W + B_W, 2 * A_W + 2 * B_W
O_P, O_Q, O_K, O_V, O_QI = 0, C_W, C_W + ATT_W, C_W + 2 * ATT_W, C_W + 3 * ATT_W
O_KI = O_QI + IDX_HEADS * IDX_DIM
O_WI = O_KI + IDX_DIM

LANES = 128
SUBLANES = 8
VMEM_LIMIT = 56 * 1024 * 1024

TM = 512
TM_POST = 256
FF_CHUNK = 256
KEY_CHUNK = 512
SCAN_CHUNK = 256
S_ROWS = DEC_SEQ * DEC_BATCH
INT_MIN = -2 ** 31
NEG_BIG = -1e30

bf16 = jnp.bfloat16
f32 = jnp.float32


def _ln(x, g, b):
    mu = jnp.mean(x, axis=-1, keepdims=True)
    xc = x - mu
    var = jnp.mean(xc * xc, axis=-1, keepdims=True)
    return xc * lax.rsqrt(var + LN_EPS) * g + b


def _gelu(x):
    c = (2.0 / jnp.pi) ** 0.5
    return x * (0.5 * (1.0 + jnp.tanh(c * (x + 0.044715 * (x * x * x)))))


def _dot(a, b):
    return jnp.dot(a, b, preferred_element_type=f32)


def _dot_nt(a, b):
    return lax.dot_general(a, b, (((1,), (1,)), ((), ())), preferred_element_type=f32)


def _prev_rows_prompt(cur, halo):
    ext = jnp.concatenate([halo, cur], axis=0)
    p1 = pltpu.roll(ext, 1, 0)[SUBLANES:]
    p2 = pltpu.roll(ext, 2, 0)[SUBLANES:]
    return p1, p2


def _prev_rows_sample(cur, hist):
    n = DEC_BATCH
    p1 = jnp.concatenate([hist[n:2 * n], cur[0:3 * n]], axis=0)
    p2 = jnp.concatenate([hist, cur[0:2 * n]], axis=0)
    return p1, p2


def _causal_conv(cur, p1, p2, w_ref, c0, width):
    return (cur * w_ref[2:3, c0:c0 + width] + p1 * w_ref[1:2, c0:c0 + width]
            + p2 * w_ref[0:1, c0:c0 + width])


def _even_pre_kernel(*refs, sample):
    if sample:
        (x_ref, w_ref, lg_ref, lb_ref, wcoef_ref, bias_ref, cw_ref, hist_ref,
         a_ref, b_ref, v_ref, e_ref) = refs
    else:
        (x_ref, w_ref, lg_ref, lb_ref, ws_ref, bias_ref, cw_ref,
         a_ref, b_ref, st_ref, halo_ref) = refs

        @pl.when(pl.program_id(1) == 0)
        def _():
            halo_ref[...] = jnp.zeros_like(halo_ref)

    xb = x_ref[...].astype(bf16)
    rows = xb.shape[0]

    def proj(c0):
        return _dot(xb, w_ref[:, c0:c0 + A_W])

    u = _gelu(proj(E_U))
    v = _ln(_gelu(proj(E_V)), lg_ref[...], lb_ref[...])
    if sample:
        v_ref[...] = v
        n = DEC_BATCH
        for t in range(DEC_SEQ):
            mix = bias_ref[t:t + 1, :]
            for s in range(t + 1):
                r = t * DEC_SEQ + s
                mix = mix + wcoef_ref[r:r + 1, :] * v[s * n:(s + 1) * n]
            a_ref[t * n:(t + 1) * n, :] = (u[t * n:(t + 1) * n] * mix).astype(bf16)
    else:
        vb = v.astype(bf16)
        ri = lax.broadcasted_iota(jnp.int32, (CHUNK, CHUNK), 0)
        ci = lax.broadcasted_iota(jnp.int32, (CHUNK, CHUNK), 1)
        for g in range(A_GROUPS):
            gs = slice(g * CHUNK, (g + 1) * CHUNK)
            wc = jnp.where(ri >= ci, ws_ref[g], 0.0).astype(bf16)
            for c in range(rows // CHUNK):
                rs = slice(c * CHUNK, (c + 1) * CHUNK)
                mix = _dot(wc, vb[rs, gs]) + bias_ref[:, gs]
                a_ref[rs, gs] = (u[rs, gs] * mix).astype(bf16)

    e = proj(E_C) * proj(E_H)
    if sample:
        e_ref[...] = e
        p1, p2 = _prev_rows_sample(e, hist_ref[...])
    else:
        p1, p2 = _prev_rows_prompt(e, halo_ref[...])
        halo_ref[...] = e[rows - SUBLANES:]
        st_ref[...] = e[rows - SUBLANES:]
    conv = _causal_conv(e, p1, p2, cw_ref, 0, B_W)
    b_ref[...] = (proj(E_B) * conv).astype(bf16)


def _full(shape):
    nd = len(shape)
    return pl.BlockSpec(shape, lambda *_: (0,) * nd)


def _even_pre_prompt(x, w, lg, lb, ws, bias, cw):
    nt = SEQ // TM
    row = lambda c: pl.BlockSpec((None, TM, c), lambda b, t: (b, t, 0))
    return pl.pallas_call(
        functools.partial(_even_pre_kernel, sample=False),
        grid=(BATCH, nt),
        in_specs=[row(D_MODEL), _full(w.shape), _full(lg.shape), _full(lb.shape), _full(ws.shape),
                  _full(bias.shape), _full(cw.shape)],
        out_specs=[row(A_W), row(B_W), pl.BlockSpec((None, SUBLANES, B_W), lambda b, t: (b, 0, 0))],
        out_shape=[jax.ShapeDtypeStruct((BATCH, SEQ, A_W), bf16),
                   jax.ShapeDtypeStruct((BATCH, SEQ, B_W), bf16),
                   jax.ShapeDtypeStruct((BATCH, SUBLANES, B_W), f32)],
        scratch_shapes=[pltpu.VMEM((SUBLANES, B_W), f32)],
        compiler_params=pltpu.CompilerParams(dimension_semantics=("arbitrary", "arbitrary"),
                                             vmem_limit_bytes=VMEM_LIMIT),
        name="even_pre_prompt",
    )(x, w, lg, lb, ws, bias, cw)


def _even_pre_sample(x, w, lg, lb, wcoef, bias, cw, hist):
    args = (x, w, lg, lb, wcoef, bias, cw, hist)
    return pl.pallas_call(
        functools.partial(_even_pre_kernel, sample=True),
        grid=(1,),
        in_specs=[_full(a.shape) for a in args],
        out_specs=[_full((S_ROWS, A_W)), _full((S_ROWS, B_W)), _full((S_ROWS, A_W)), _full((S_ROWS, B_W))],
        out_shape=[jax.ShapeDtypeStruct((S_ROWS, A_W), bf16),
                   jax.ShapeDtypeStruct((S_ROWS, B_W), bf16),
                   jax.ShapeDtypeStruct((S_ROWS, A_W), f32),
                   jax.ShapeDtypeStruct((S_ROWS, B_W), f32)],
        compiler_params=pltpu.CompilerParams(dimension_semantics=("arbitrary",),
                                             vmem_limit_bytes=VMEM_LIMIT),
        name="even_pre_sample",
    )(*args)


def _post_kernel(*refs, sample):
    if sample:
        (x_ref, ca_ref, cb_ref, wo_ref, g1_ref, b1_ref, wu_ref, cw_ref, cb2_ref, wd_ref, g2_ref, b2_ref,
         hist_ref, o_ref, st_ref, act_ref) = refs
    else:
        (x_ref, ca_ref, cb_ref, wo_ref, g1_ref, b1_ref, wu_ref, cw_ref, cb2_ref, wd_ref, g2_ref, b2_ref,
         o_ref, st_ref, act_ref, halo_ref) = refs

        @pl.when(pl.program_id(1) == 0)
        def _():
            halo_ref[...] = jnp.zeros_like(halo_ref)

    half = ca_ref.shape[-1]
    y = _dot(ca_ref[...], wo_ref[0:half, :]) + _dot(cb_ref[...], wo_ref[half:2 * half, :])
    x1 = _ln(ALPHA * x_ref[...] + y, g1_ref[...], b1_ref[...])
    x1b = x1.astype(bf16)
    rows = x1b.shape[0]

    def up(c0):
        return (_dot(x1b, wu_ref[:, c0:c0 + FF_CHUNK]), _dot(x1b, wu_ref[:, D_FF + c0:D_FF + c0 + FF_CHUNK]))

    def conv_half(hu, c0):
        if sample:
            p1, p2 = _prev_rows_sample(hu, hist_ref[:, c0:c0 + FF_CHUNK])
            st_ref[:, c0:c0 + FF_CHUNK] = hu[2 * DEC_BATCH:]
        else:
            p1, p2 = _prev_rows_prompt(hu, halo_ref[:, c0:c0 + FF_CHUNK])
            halo_ref[:, c0:c0 + FF_CHUNK] = hu[rows - SUBLANES:]
            st_ref[:, c0:c0 + FF_CHUNK] = hu[rows - SUBLANES:]
        return _causal_conv(hu, p1, p2, cw_ref, c0, FF_CHUNK) + cb2_ref[:, c0:c0 + FF_CHUNK]

    n_ff = D_FF // FF_CHUNK
    nxt = up(0)
    for c in range(n_ff):
        c0 = c * FF_CHUNK
        cur = nxt
        if c + 1 < n_ff:
            nxt = up(c0 + FF_CHUNK)
        hg = conv_half(cur[0], c0)
        hv = conv_half(cur[1], D_FF + c0)
        act_ref[:, c0:c0 + FF_CHUNK] = (hg * jax.nn.sigmoid(hg) * hv).astype(bf16)
    o_ref[...] = _ln(ALPHA * x1 + _dot(act_ref[...], wd_ref[...]), g2_ref[...], b2_ref[...])


def _layer_spec(a, layer, **kw):
    return pl.BlockSpec((None,) + a.shape[1:], lambda *_: (layer,) + (0,) * (a.ndim - 1), **kw)


def _post_prompt(layer, x, ca, cb, wo, g1, b1, wu, cw, cb2, wd, g2, b2):
    nt = SEQ // TM_POST
    row = lambda c: pl.BlockSpec((None, TM_POST, c), lambda b, t: (b, t, 0))
    const = lambda a: pl.BlockSpec(a.shape, lambda b, t: (0,) * a.ndim, pipeline_mode=pl.Buffered(1))
    stacked = lambda a: _layer_spec(a, layer, pipeline_mode=pl.Buffered(1))
    return pl.pallas_call(
        functools.partial(_post_kernel, sample=False),
        grid=(BATCH, nt),
        in_specs=[row(D_MODEL), row(ca.shape[-1]), row(cb.shape[-1]), const(wo)]
                 + [stacked(a) for a in (g1, b1, wu, cw, cb2, wd, g2, b2)],
        out_specs=[row(D_MODEL), pl.BlockSpec((None, SUBLANES, 2 * D_FF), lambda b, t: (b, 0, 0))],
        out_shape=[jax.ShapeDtypeStruct((BATCH, SEQ, D_MODEL), f32),
                   jax.ShapeDtypeStruct((BATCH, SUBLANES, 2 * D_FF), f32)],
        scratch_shapes=[pltpu.VMEM((TM_POST, D_FF), bf16), pltpu.VMEM((SUBLANES, 2 * D_FF), f32)],
        compiler_params=pltpu.CompilerParams(dimension_semantics=("arbitrary", "arbitrary"),
                                             vmem_limit_bytes=VMEM_LIMIT),
        name="post_prompt",
    )(x, ca, cb, wo, g1, b1, wu, cw, cb2, wd, g2, b2)


def _post_sample(layer, x, ca, cb, wo, g1, b1, wu, cw, cb2, wd, g2, b2, hist):
    args = (x, ca, cb, wo, g1, b1, wu, cw, cb2, wd, g2, b2, hist)
    full = lambda a: _full(a.shape)
    stacked = lambda a: _layer_spec(a, layer)
    return pl.pallas_call(
        functools.partial(_post_kernel, sample=True),
        grid=(1,),
        in_specs=[full(x), full(ca), full(cb), full(wo)]
                 + [stacked(a) for a in (g1, b1, wu, cw, cb2, wd, g2, b2)] + [full(hist)],
        out_specs=[_full((S_ROWS, D_MODEL)), _full((2 * DEC_BATCH, 2 * D_FF))],
        out_shape=[jax.ShapeDtypeStruct((S_ROWS, D_MODEL), f32),
                   jax.ShapeDtypeStruct((2 * DEC_BATCH, 2 * D_FF), f32)],
        scratch_shapes=[pltpu.VMEM((S_ROWS, D_FF), bf16)],
        compiler_params=pltpu.CompilerParams(dimension_semantics=("arbitrary",),
                                             vmem_limit_bytes=VMEM_LIMIT),
        name="post_sample",
    )(*args)


def _rope(blk, cos, s1, s2):
    return blk * cos + pltpu.roll(blk, ROT // 2, 1) * s1 + pltpu.roll(blk, LANES - ROT // 2, 1) * s2


def _odd_pre_kernel(*refs, sample):
    if sample:
        (x_ref, wm_ref, wt_ref, cos_ref, s1_ref, s2_ref, cwt_ref, cs_ref, hist_ref,
         k_ref, v_ref, ki_ref, q_ref, qi_ref, wi_ref, co_ref, p_ref) = refs
    else:
        (x_ref, wm_ref, wt_ref, cos_ref, s1_ref, s2_ref, cwt_ref, cs_ref,
         k_ref, v_ref, ki_ref, qb_ref, kb_ref, va_ref, qit_ref, kib_ref, wit_ref, co_ref, ph_ref,
         halo_ref) = refs

        @pl.when(pl.program_id(1) == 0)
        def _():
            halo_ref[...] = jnp.zeros_like(halo_ref)

    xb = x_ref[...].astype(bf16)
    rows = xb.shape[0]
    cos, s1, s2 = cos_ref[...], s1_ref[...], s2_ref[...]

    def proj(c0):
        return _dot(xb, wm_ref[:, c0:c0 + C_W])

    def rope_wide(h):
        return jnp.concatenate(
            [_rope(h[:, j * LANES:(j + 1) * LANES], cos, s1, s2) for j in range(h.shape[1] // LANES)], axis=1)

    q = rope_wide(proj(O_Q))
    k = rope_wide(proj(O_K))
    v = proj(O_V)
    qi = rope_wide(proj(O_QI))
    tail = _dot(xb, wt_ref[...])
    ki2 = _rope(tail[:, 0:LANES], cos, s1, s2)
    k_ref[...] = k
    v_ref[...] = v
    ki_ref[...] = ki2[:, 0:IDX_DIM]
    wi = tail[:, LANES:2 * LANES]
    if sample:
        q_ref[...] = q
        qi_ref[...] = qi
        wi_ref[...] = wi
    else:
        qb_ref[...] = (q * (ATT_SCALE * LOG2E)).astype(bf16)
        kb_ref[...] = k.astype(bf16)
        low = lax.broadcasted_iota(jnp.int32, (1, LANES), 1) < HEAD_DIM
        for j in range(N_HEADS // 2):
            vj = v[:, j * LANES:(j + 1) * LANES]
            va_ref[:, 2 * j * LANES:(2 * j + 1) * LANES] = jnp.where(low, vj, 1.0).astype(bf16)
            va_ref[:, (2 * j + 1) * LANES:(2 * j + 2) * LANES] = jnp.where(low, 1.0, vj).astype(bf16)
        qit_ref[...] = qi.T.astype(bf16)
        kib_ref[...] = ki2[:, 0:IDX_DIM].astype(bf16)
        wit_ref[...] = wi.T[0:IDX_HEADS]

    p = proj(O_P)
    if sample:
        p_ref[...] = p
        n = DEC_BATCH
        hist = hist_ref[...]

        def slab(j, gs):
            if j < POOL_HIST:
                return hist[j * n:(j + 1) * n, gs]
            return p[(j - POOL_HIST) * n:(j - POOL_HIST + 1) * n, gs]

        for g, w in enumerate(POOL_WINDOWS):
            gs = slice(g * C_GROUP_W, (g + 1) * C_GROUP_W)
            for t in range(DEC_SEQ):
                win = slab(POOL_HIST + t, gs)
                for i in range(1, w):
                    win = win + slab(POOL_HIST + t - i, gs)
                pooled = win / float(w) - p[t * n:(t + 1) * n, gs]
                co = _dot(pooled.astype(bf16), cwt_ref[g]) * cs_ref[:, gs]
                co_ref[t * n:(t + 1) * n, gs] = co.astype(bf16)
    else:
        hrows = 2 * SUBLANES
        ext = jnp.concatenate([halo_ref[...], p], axis=0)
        pos = pl.program_id(1) * rows + lax.broadcasted_iota(jnp.int32, (rows, 1), 0)
        for g, w in enumerate(POOL_WINDOWS):
            gs = slice(g * C_GROUP_W, (g + 1) * C_GROUP_W)
            s = ext[:, gs]
            step = 1
            while step < w:
                s = s + pltpu.roll(s, step, 0)
                step *= 2
            cnt = jnp.minimum(w, pos + 1).astype(f32)
            pooled = s[hrows:] / cnt - p[:, gs]
            co = _dot(pooled.astype(bf16), cwt_ref[g]) * cs_ref[:, gs]
            co_ref[:, gs] = co.astype(bf16)
        halo_ref[...] = p[rows - hrows:]
        ph_ref[...] = p[rows - hrows:]


def _odd_pre_prompt(x, wm, wt, cos, s1, s2, cwt, cs):
    nt = SEQ // TM
    row = lambda c: pl.BlockSpec((None, TM, c), lambda b, t: (b, t, 0))
    col = lambda r: pl.BlockSpec((None, r, TM), lambda b, t: (b, 0, t))
    tab = pl.BlockSpec((TM, LANES), lambda b, t: (t, 0))
    sds = lambda c, dt: jax.ShapeDtypeStruct((BATCH, SEQ, c), dt)
    sdt = lambda r, dt: jax.ShapeDtypeStruct((BATCH, r, SEQ), dt)
    hrows = 2 * SUBLANES
    return pl.pallas_call(
        functools.partial(_odd_pre_kernel, sample=False),
        grid=(BATCH, nt),
        in_specs=[row(D_MODEL), _full(wm.shape), _full(wt.shape), tab, tab, tab, _full(cwt.shape),
                  _full(cs.shape)],
        out_specs=[row(ATT_W), row(ATT_W), row(IDX_DIM), row(ATT_W), row(ATT_W), row(2 * ATT_W),
                   col(IDX_HEADS * IDX_DIM), row(IDX_DIM), col(IDX_HEADS), row(C_W),
                   pl.BlockSpec((None, hrows, C_W), lambda b, t: (b, 0, 0))],
        out_shape=[sds(ATT_W, f32), sds(ATT_W, f32), sds(IDX_DIM, f32), sds(ATT_W, bf16), sds(ATT_W, bf16),
                   sds(2 * ATT_W, bf16), sdt(IDX_HEADS * IDX_DIM, bf16), sds(IDX_DIM, bf16), sdt(IDX_HEADS, f32),
                   sds(C_W, bf16), jax.ShapeDtypeStruct((BATCH, hrows, C_W), f32)],
        scratch_shapes=[pltpu.VMEM((hrows, C_W), f32)],
        compiler_params=pltpu.CompilerParams(dimension_semantics=("arbitrary", "arbitrary"),
                                             vmem_limit_bytes=VMEM_LIMIT),
        name="odd_pre_prompt",
    )(x, wm, wt, cos, s1, s2, cwt, cs)


def _odd_pre_sample(x, wm, wt, cos, s1, s2, cwt, cs, hist):
    args = (x, wm, wt, cos, s1, s2, cwt, cs, hist)
    sds = lambda c, dt: jax.ShapeDtypeStruct((S_ROWS, c), dt)
    outs = [sds(ATT_W, f32), sds(ATT_W, f32), sds(IDX_DIM, f32), sds(ATT_W, f32), sds(ATT_W, f32),
            sds(LANES, f32), sds(C_W, bf16), sds(C_W, f32)]
    return pl.pallas_call(
        functools.partial(_odd_pre_kernel, sample=True),
        grid=(1,),
        in_specs=[_full(a.shape) for a in args],
        out_specs=[_full(o.shape) for o in outs],
        out_shape=outs,
        compiler_params=pltpu.CompilerParams(dimension_semantics=("arbitrary",),
                                             vmem_limit_bytes=VMEM_LIMIT),
        name="odd_pre_sample",
    )(*args)


def _order_key(s):
    s = jnp.where(s == 0.0, 0.0, s)
    bits = lax.bitcast_convert_type(s, jnp.int32)
    return bits ^ ((bits >> 31) & 0x7FFFFFFF)


def _dsa_prompt_kernel(q_ref, qit_ref, wit_ref, k_ref, va_ref, ki_ref, o_ref,
                       keys_ref, tpos_ref, m_ref, acc_ref, lga_ref, lgb_ref, da_ref, db_ref):
    qb = pl.program_id(1)
    n_chunks = qb // (KEY_CHUNK // Q_BLOCK) + 1
    qpos = qb * Q_BLOCK + lax.broadcasted_iota(jnp.int32, (1, Q_BLOCK), 1)
    krow = lax.broadcasted_iota(jnp.int32, (KEY_CHUNK, 1), 0)
    kf = float(TOPK)

    qit = qit_ref[...]
    stat = jnp.concatenate([qit[h * IDX_DIM:(h + 1) * IDX_DIM, :] for h in range(IDX_HEADS)], axis=1)
    wit = wit_ref[...]

    def idx_dots(c, d_ref):
        ks = pl.multiple_of(c * KEY_CHUNK, KEY_CHUNK)
        d_ref[...] = _dot(ki_ref[pl.ds(ks, KEY_CHUNK), :], stat)

    def idx_keys(c, d_ref):
        ks = pl.multiple_of(c * KEY_CHUNK, KEY_CHUNK)
        s = jnp.zeros((KEY_CHUNK, Q_BLOCK), f32)
        for h in range(IDX_HEADS):
            s = s + wit[h:h + 1, :] * jnp.maximum(d_ref[:, h * Q_BLOCK:(h + 1) * Q_BLOCK], 0.0)
        keys_ref[pl.ds(ks, KEY_CHUNK), :] = jnp.where((ks + krow) <= qpos, _order_key(s), INT_MIN)

    def two_chunk_pipeline(produce, consume, buf_a, buf_b):
        last = n_chunks - 1
        produce(0, buf_a)

        def body(i, carry):
            produce(2 * i + 1, buf_b)
            consume(2 * i, buf_a)
            produce(jnp.minimum(2 * i + 2, last), buf_a)
            consume(2 * i + 1, buf_b)
            return carry

        lax.fori_loop(0, n_chunks // 2, body, 0)

        @pl.when(lax.rem(n_chunks, 2) == 1)
        def _():
            consume(last, buf_a)

    two_chunk_pipeline(idx_dots, idx_keys, da_ref, db_ref)

    def count(pred, trips, width):
        lanes_acc = 4

        def body(c, accs):
            ks = c * width if isinstance(c, int) else pl.multiple_of(c * width, width)
            m3 = jnp.where(pred(ks, width), 1.0, 0.0).reshape(width // SUBLANES, SUBLANES, LANES)
            accs = list(accs)
            for i in range(width // SUBLANES):
                accs[i % lanes_acc] = accs[i % lanes_acc] + m3[i]
            return tuple(accs)

        accs = (jnp.zeros((SUBLANES, LANES), f32),) * lanes_acc
        if isinstance(trips, int):
            for c in range(trips):
                accs = body(c, accs)
        else:
            accs = lax.fori_loop(0, trips, body, accs)
        return jnp.sum((accs[0] + accs[1]) + (accs[2] + accs[3]), axis=0, keepdims=True)

    def select(n_scan):
        def cnt_ge(cand):
            return count(lambda ks, w: keys_ref[pl.ds(ks, w), :] >= cand, n_scan, SCAN_CHUNK)

        zero = jnp.zeros((1, LANES), jnp.int32)
        c0 = cnt_ge(zero)
        ok0 = c0 >= kf
        t0 = jnp.where(ok0, zero, jnp.full((1, LANES), INT_MIN, jnp.int32))
        ct0 = jnp.where(ok0, c0, jnp.full((1, LANES), 1e9, f32))

        def body(i, st):
            t, ct = st
            cand = t | lax.shift_left(jnp.int32(1), 30 - i)
            c = cnt_ge(cand)
            ok = c >= kf
            return jnp.where(ok, cand, t), jnp.where(ok, c, ct)

        thr, cthr = lax.fori_loop(0, 31, body, (t0, ct0))
        settled = jnp.logical_or(cthr == kf, thr == INT_MIN)

        def tie_search():
            need = kf - count(lambda ks, w: keys_ref[pl.ds(ks, w), :] > thr, n_scan, SCAN_CHUNK)
            far = jnp.int32(2 * SEQ)
            for c in range(n_scan):
                rs = pl.ds(c * SCAN_CHUNK, SCAN_CHUNK)
                pos = c * SCAN_CHUNK + lax.broadcasted_iota(jnp.int32, (SCAN_CHUNK, LANES), 0)
                tpos_ref[rs, :] = jnp.where(keys_ref[rs, :] == thr, pos, far)
            bits = (n_scan * SCAN_CHUNK - 1).bit_length() + 1

            def jbit(i, j):
                cand = j | lax.shift_left(jnp.int32(1), bits - 1 - i)
                c = count(lambda ks, w: tpos_ref[pl.ds(ks, w), :] < cand, n_scan, SCAN_CHUNK)
                return jnp.where(c <= need, cand, j)

            return lax.fori_loop(0, bits, jbit, zero)

        has_tie = jnp.max(jnp.where(settled, 0.0, 1.0)) > 0.0
        jsel = lax.cond(has_tie, tie_search, lambda: jnp.full((1, LANES), 2 * SEQ, jnp.int32))
        return jnp.maximum(thr, INT_MIN + 1), jsel

    n_scan = (qb * Q_BLOCK + Q_BLOCK + SCAN_CHUNK - 1) // SCAN_CHUNK
    thr, jsel = lax.switch(
        n_scan - 1, [functools.partial(select, k) for k in range(1, SEQ // SCAN_CHUNK + 1)])

    low = lax.broadcasted_iota(jnp.int32, (1, LANES), 1) < HEAD_DIM
    pairs = N_HEADS // 2
    m_ref[...] = jnp.full(m_ref.shape, NEG_BIG, f32)
    acc_ref[...] = jnp.zeros_like(acc_ref)
    q = q_ref[...]
    eye = (lax.broadcasted_iota(jnp.int32, (Q_BLOCK, Q_BLOCK), 0)
           == lax.broadcasted_iota(jnp.int32, (Q_BLOCK, Q_BLOCK), 1)).astype(bf16)
    qpair = []
    for j in range(pairs):
        blk = q[:, j * LANES:(j + 1) * LANES]
        nil = jnp.zeros_like(blk)
        rows2 = jnp.concatenate([jnp.where(low, blk, nil), jnp.where(low, nil, blk)], axis=0)
        qpair.append(jnp.concatenate([rows2, jnp.concatenate([eye, eye], axis=0)], axis=1))

    def qk_all(c, lg_ref):
        ks = pl.multiple_of(c * KEY_CHUNK, KEY_CHUNK)
        kk = keys_ref[pl.ds(ks, KEY_CHUNK), :]
        pos = ks + lax.broadcasted_iota(jnp.int32, (KEY_CHUNK, LANES), 0)
        b = jnp.where(kk > thr, 0.0, jnp.where(kk == thr, jnp.where(pos < jsel, 0.0, NEG_BIG), NEG_BIG))
        bias_t = b.astype(bf16)
        for j in range(pairs):
            rhs = jnp.concatenate([k_ref[pl.ds(ks, KEY_CHUNK), j * LANES:(j + 1) * LANES], bias_t], axis=1)
            lg_ref[j] = _dot_nt(qpair[j], rhs)

    def soft_pv(c, lg_ref):
        ks = pl.multiple_of(c * KEY_CHUNK, KEY_CHUNK)
        for j in range(pairs):
            lg = lg_ref[j]
            m_old = m_ref[j]
            m_new = jnp.maximum(m_old, jnp.max(lg, axis=1, keepdims=True))
            p = jnp.exp2(lg - m_new).astype(bf16)
            a = jnp.exp2(m_old - m_new)
            pv = _dot(p, va_ref[pl.ds(ks, KEY_CHUNK), 2 * j * LANES:(2 * j + 2) * LANES])
            own = jnp.concatenate([pv[0:Q_BLOCK, 0:LANES], pv[Q_BLOCK:2 * Q_BLOCK, LANES:2 * LANES]], axis=0)
            acc_ref[j] = a * acc_ref[j] + own
            m_ref[j] = m_new

    two_chunk_pipeline(qk_all, soft_pv, lga_ref, lgb_ref)

    for j in range(pairs):
        acc = acc_ref[j]
        out = acc / pltpu.roll(acc, HEAD_DIM, 1)
        o_ref[:, j * LANES:(j + 1) * LANES] = jnp.where(low, out[0:Q_BLOCK], out[Q_BLOCK:2 * Q_BLOCK]).astype(bf16)


def _dsa_prompt(qb, qit, wit, kb, va, kib):
    nq = SEQ // Q_BLOCK
    qrow = lambda c: pl.BlockSpec((None, Q_BLOCK, c), lambda b, t: (b, t, 0))
    qcol = lambda r: pl.BlockSpec((None, r, Q_BLOCK), lambda b, t: (b, 0, t))
    seq = lambda c: pl.BlockSpec((None, SEQ, c), lambda b, t: (b, 0, 0))
    return pl.pallas_call(
        _dsa_prompt_kernel,
        grid=(BATCH, nq),
        in_specs=[qrow(ATT_W), qcol(IDX_HEADS * IDX_DIM), qcol(IDX_HEADS), seq(ATT_W), seq(2 * ATT_W),
                  seq(IDX_DIM)],
        out_specs=qrow(ATT_W),
        out_shape=jax.ShapeDtypeStruct((BATCH, SEQ, ATT_W), bf16),
        scratch_shapes=[pltpu.VMEM((SEQ, Q_BLOCK), jnp.int32),
                        pltpu.VMEM((SEQ, Q_BLOCK), jnp.int32),
                        pltpu.VMEM((N_HEADS // 2, 2 * Q_BLOCK, 1), f32),
                        pltpu.VMEM((N_HEADS // 2, 2 * Q_BLOCK, LANES), f32),
                        pltpu.VMEM((N_HEADS // 2, 2 * Q_BLOCK, KEY_CHUNK), f32),
                        pltpu.VMEM((N_HEADS // 2, 2 * Q_BLOCK, KEY_CHUNK), f32),
                        pltpu.VMEM((KEY_CHUNK, IDX_HEADS * Q_BLOCK), f32),
                        pltpu.VMEM((KEY_CHUNK, IDX_HEADS * Q_BLOCK), f32)],
        compiler_params=pltpu.CompilerParams(dimension_semantics=("arbitrary", "arbitrary"),
                                             vmem_limit_bytes=VMEM_LIMIT),
        name="dsa_prompt",
    )(qb, qit, wit, kb, va, kib)


S_KEYS = PAST_LEN + PAGE_SIZE
BITS_PER_PASS = 4
SEQ_PER_STEP = 2


def _dsa_sample_kernel(pt_ref, q_ref, qi_ref, wi_ref, kin_ref, kn_ref, vn_ref, *rest):
    g_n = SEQ_PER_STEP
    kidx_refs = rest[0:g_n * N_PAGES]
    k_refs = rest[g_n * N_PAGES:2 * g_n * N_PAGES]
    v_refs = rest[2 * g_n * N_PAGES:3 * g_n * N_PAGES]
    o_ref = rest[3 * g_n * N_PAGES]
    del pt_ref
    rows = DEC_SEQ * IDX_HEADS
    kf = float(TOPK)
    kpos = lax.broadcasted_iota(jnp.int32, (1, S_KEYS), 1)
    tq = lax.broadcasted_iota(jnp.int32, (DEC_SEQ, 1), 0)

    def pad_keys(x):
        return jnp.concatenate([x, jnp.zeros((PAGE_SIZE - x.shape[0], x.shape[1]), x.dtype)], axis=0)

    def count(pred):
        return jnp.sum(jnp.where(pred, 1.0, 0.0), axis=1, keepdims=True)

    def order_keys(g):
        qi = qi_ref[g].astype(bf16)
        wi = wi_ref[g]

        def idx_score(d):
            r = jnp.maximum(d, 0.0) * wi
            return jnp.sum(r.reshape(DEC_SEQ, IDX_HEADS, PAGE_SIZE), axis=1)

        parts = [idx_score(_dot(qi, kidx_refs[g * N_PAGES + j][...].astype(bf16))) for j in range(N_PAGES)]
        parts.append(idx_score(_dot_nt(qi, pad_keys(kin_ref[g]).astype(bf16))))
        s = jnp.concatenate(parts, axis=1)
        return jnp.where((kpos - PAST_LEN) <= tq, _order_key(s), INT_MIN)

    hrow = lax.broadcasted_iota(jnp.int32, (rows, ATT_W), 0) % N_HEADS
    hcol = lax.broadcasted_iota(jnp.int32, (rows, ATT_W), 1) // HEAD_DIM
    own = hrow == hcol

    def raw_logits(g):
        q4 = q_ref[g] * ATT_SCALE
        q32 = jnp.broadcast_to(q4[:, None, :], (DEC_SEQ, N_HEADS, ATT_W)).reshape(rows, ATT_W)
        qbd = jnp.where(own, q32, 0.0).astype(bf16)
        kn = pad_keys(kn_ref[g]).astype(bf16)
        return jnp.concatenate([_dot(qbd, k_refs[g * N_PAGES + j][...].astype(bf16)) for j in range(N_PAGES)]
                               + [_dot_nt(qbd, kn)], axis=1)

    lg_raw = [raw_logits(g) for g in range(g_n)]

    keys = jnp.concatenate([order_keys(g) for g in range(g_n)], axis=0)
    sel_rows = g_n * DEC_SEQ
    zero = jnp.zeros((sel_rows, 1), jnp.int32)
    c0 = count(keys >= zero)
    thr = jnp.where(c0 >= kf, zero, jnp.full((sel_rows, 1), INT_MIN, jnp.int32))
    cthr = jnp.where(c0 >= kf, c0, 1e9)
    shift = 31
    while shift > 0:
        nb = min(BITS_PER_PASS, shift)
        shift -= nb
        num = zero
        for c in range(1, 2 ** nb):
            cnt = count(keys >= (thr | (c << shift)))
            ok = cnt >= kf
            num = num + ok.astype(jnp.int32)
            cthr = jnp.where(ok, cnt, cthr)
        thr = thr | lax.shift_left(num, shift)
    need = kf - count(keys > thr)
    settled = jnp.logical_or(cthr == kf, thr == INT_MIN)

    def tie_search():
        def jbit(i, j):
            cand = j | lax.shift_left(jnp.int32(1), 12 - i)
            c = count(jnp.logical_and(keys == thr, kpos < cand))
            return jnp.where(c <= need, cand, j)
        return lax.fori_loop(0, 13, jbit, zero)

    jsel = lax.cond(jnp.max(jnp.where(settled, 0.0, 1.0)) > 0.0, tie_search,
                    lambda: jnp.full((sel_rows, 1), 2 * S_KEYS, jnp.int32))
    sel_all = jnp.logical_and(
        jnp.logical_or(keys > thr, jnp.logical_and(keys == thr, kpos < jsel)), keys != INT_MIN)
    bias_all = jnp.where(sel_all, 0.0, NEG_BIG)

    for g in range(g_n):
        bias4 = bias_all[g * DEC_SEQ:(g + 1) * DEC_SEQ]
        bias = jnp.broadcast_to(bias4[:, None, :], (DEC_SEQ, N_HEADS, S_KEYS)).reshape(rows, S_KEYS)
        lg = lg_raw[g] + bias
        m = jnp.max(lg, axis=1, keepdims=True)
        p = jnp.exp(lg - m)
        l = jnp.sum(p, axis=1, keepdims=True)
        pb = p.astype(bf16)
        pv = _dot(pb[:, PAST_LEN:], pad_keys(vn_ref[g]).astype(bf16))
        for j in range(N_PAGES):
            pv = pv + _dot_nt(pb[:, j * PAGE_SIZE:(j + 1) * PAGE_SIZE], v_refs[g * N_PAGES + j][...].astype(bf16))
        out = jnp.where(own, pv, 0.0) / l
        o_ref[g] = jnp.sum(out.reshape(DEC_SEQ, N_HEADS, ATT_W), axis=1)


def _dsa_sample(page_table, q, qi, wi, ki_new, k_new, v_new, kidx_pages, k_pages, v_pages, layer):
    n_pool = kidx_pages.shape[0] // (DEPTH // 2)
    rows = DEC_SEQ * IDX_HEADS
    g_n = SEQ_PER_STEP
    per_seq = lambda r, c: pl.BlockSpec((g_n, r, c), lambda s, pt: (s, 0, 0))

    def page(r, g, j):
        return pl.BlockSpec((None, r, PAGE_SIZE), lambda s, pt: (layer * n_pool + pt[s * g_n + g, j], 0, 0))

    pages = lambda r: [page(r, g, j) for g in range(g_n) for j in range(N_PAGES)]
    in_specs = [per_seq(DEC_SEQ, ATT_W), per_seq(rows, IDX_DIM), per_seq(rows, 1),
                per_seq(SUBLANES, IDX_DIM), per_seq(SUBLANES, ATT_W), per_seq(SUBLANES, ATT_W)]
    in_specs += pages(IDX_DIM) + pages(ATT_W) + pages(ATT_W)
    grid_spec = pltpu.PrefetchScalarGridSpec(
        num_scalar_prefetch=1, grid=(DEC_BATCH // g_n,), in_specs=in_specs,
        out_specs=per_seq(DEC_SEQ, ATT_W))
    n_in = g_n * N_PAGES
    return pl.pallas_call(
        _dsa_sample_kernel,
        grid_spec=grid_spec,
        out_shape=jax.ShapeDtypeStruct((DEC_BATCH, DEC_SEQ, ATT_W), f32),
        compiler_params=pltpu.CompilerParams(dimension_semantics=("arbitrary",),
                                             vmem_limit_bytes=VMEM_LIMIT),
        name="dsa_sample",
    )(page_table, q, qi, wi, ki_new, k_new, v_new,
      *([kidx_pages] * n_in), *([k_pages] * n_in), *([v_pages] * n_in))


def _rope_tables(pos):
    half = ROT // 2
    inv = np.power(np.float32(ROPE_THETA), -np.arange(half, dtype=np.float32) * np.float32(2.0) / np.float32(ROT))
    ang = np.asarray(pos, np.float32)[:, None] * inv.astype(np.float32)[None, :]
    cos, sin = np.cos(ang).astype(np.float32), np.sin(ang).astype(np.float32)
    t = len(pos)
    rest = HEAD_DIM - ROT
    c64 = np.concatenate([cos, cos, np.ones((t, rest), np.float32)], axis=1)
    s1 = np.concatenate([np.zeros((t, half), np.float32), sin, np.zeros((t, rest), np.float32)], axis=1)
    s2 = np.concatenate([-sin, np.zeros((t, half + rest), np.float32)], axis=1)
    two = lambda a: jnp.asarray(np.concatenate([a, a], axis=1))
    return two(c64), two(s1), two(s2)


def _to_tm(a):
    return a.transpose(1, 0, 2).reshape(a.shape[1] * a.shape[0], a.shape[2])


def _from_tm(a, t=DEC_SEQ):
    return a.reshape(t, DEC_BATCH, a.shape[-1]).transpose(1, 0, 2)


def kernel(x_prompt, x_sample, state_b_conv, state_c_pool, cache_k, cache_v, cache_kidx, state_ffn_conv, page_table, w_in_even, a_ln_g, a_ln_b, a_ws, a_bs, b_conv_w, w_out_even, w_in_odd, c_w, c_scale, w_out_odd, ln_mix_g, ln_mix_b, ffn_w_up, ffn_conv_w, ffn_conv_b, ffn_w_down, ln_ffn_g, ln_ffn_b):
    row = lambda a: a.reshape(1, -1)
    xp = x_prompt
    xs = _to_tm(x_sample)
    outs = {}
    ffn_p, ffn_s = [], []

    rows3 = lambda a: a[:, None, :]
    ffn_wts = (rows3(ln_mix_g), rows3(ln_mix_b), ffn_w_up.astype(bf16), ffn_conv_w, rows3(ffn_conv_b),
               ffn_w_down.astype(bf16), rows3(ln_ffn_g), rows3(ln_ffn_b))

    def post(layer, xp, xs, cap, cbp, cas, cbs, wo):
        wob = wo.astype(bf16)
        xp, st_p = _post_prompt(layer, xp, cap, cbp, wob, *ffn_wts)
        xs, st_s = _post_sample(layer, xs, cas, cbs, wob, *ffn_wts, _to_tm(state_ffn_conv[layer]))
        ffn_p.append(st_p[:, SUBLANES - (CONV_W - 1):])
        ffn_s.append(_from_tm(st_s, CONV_W - 1))
        return xp, xs

    for layer in range(DEPTH):
        if layer % 2 == 0:
            e = layer // 2
            w = w_in_even[e].astype(bf16)
            lg, lb = row(a_ln_g[e]), row(a_ln_b[e])
            bias = jnp.repeat(a_bs[e].T, CHUNK, axis=1)
            wcoef = jnp.repeat(a_ws[e][:, :DEC_SEQ, :DEC_SEQ].transpose(1, 2, 0).reshape(DEC_SEQ * DEC_SEQ, A_GROUPS),
                               CHUNK, axis=1)
            a_p, b_p, bst_p = _even_pre_prompt(xp, w, lg, lb, a_ws[e], bias, b_conv_w[e])
            a_s, b_s, v_s, e_s = _even_pre_sample(xs, w, lg, lb, wcoef, bias, b_conv_w[e],
                                                  _to_tm(state_b_conv[e]))
            outs.setdefault("a_v_s", []).append(_from_tm(v_s))
            outs.setdefault("b_p", []).append(bst_p[:, SUBLANES - (CONV_W - 1):])
            outs.setdefault("b_s", []).append(_from_tm(e_s)[:, DEC_SEQ - (CONV_W - 1):])
            xp, xs = post(layer, xp, xs, a_p, b_p, a_s, b_s, w_out_even[e])
        else:
            o = layer // 2
            wm = w_in_odd[o][:, :O_KI].astype(bf16)
            wki = w_in_odd[o][:, O_KI:O_WI]
            wwi = w_in_odd[o][:, O_WI:]
            wt = jnp.concatenate([wki, wki, wwi, jnp.zeros((D_MODEL, LANES - IDX_HEADS), f32)],
                                 axis=1).astype(bf16)
            cwt = c_w[o].astype(bf16)
            cs = row(c_scale[o])
            tabs_p = _rope_tables(np.arange(SEQ))
            tabs_s = _rope_tables(PAST_LEN + np.repeat(np.arange(DEC_SEQ), DEC_BATCH))
            (k_p, v_p, ki_p, qb, kb, va, qit, kib, wit, co_p, ph_p) = _odd_pre_prompt(
                xp, wm, wt, *tabs_p, cwt, cs)
            att_p = _dsa_prompt(qb, qit, wit, kb, va, kib)
            (k_s, v_s2, ki_s, q_s, qi_s, wi_s, co_s, p_s) = _odd_pre_sample(
                xs, wm, wt, *tabs_s, cwt, cs, _to_tm(state_c_pool[o]))
            rows = DEC_SEQ * IDX_HEADS
            pad8 = lambda a: jnp.pad(_from_tm(a), ((0, 0), (0, SUBLANES - DEC_SEQ), (0, 0)))
            page_t = lambda c, w: jnp.moveaxis(c, 2, -1).reshape(-1, w, PAGE_SIZE)
            att_s = _dsa_sample(
                page_table,
                _from_tm(q_s),
                _from_tm(qi_s).reshape(DEC_BATCH, rows, IDX_DIM),
                _from_tm(wi_s)[:, :, :IDX_HEADS].reshape(DEC_BATCH, rows, 1),
                pad8(ki_s), pad8(k_s), pad8(v_s2),
                page_t(cache_kidx, IDX_DIM), page_t(cache_k, ATT_W), page_t(cache_v, ATT_W), o)
            heads = lambda a: a.reshape(a.shape[0], a.shape[1], N_HEADS, HEAD_DIM)
            outs.setdefault("c_p", []).append(ph_p[:, 2 * SUBLANES - POOL_HIST:])
            outs.setdefault("c_s", []).append(
                jnp.concatenate([state_c_pool[o][:, DEC_SEQ:], _from_tm(p_s)], axis=1))
            outs.setdefault("k_p", []).append(heads(k_p))
            outs.setdefault("v_p", []).append(heads(v_p))
            outs.setdefault("ki_p", []).append(ki_p)
            outs.setdefault("k_s", []).append(heads(_from_tm(k_s)))
            outs.setdefault("v_s", []).append(heads(_from_tm(v_s2)))
            outs.setdefault("ki_s", []).append(_from_tm(ki_s))
            xp, xs = post(layer, xp, xs, co_p, att_p, co_s, _to_tm(att_s).astype(bf16), w_out_odd[o])

    st = lambda name: jnp.stack(outs[name])
    return (xp, _from_tm(xs), st("a_v_s"), st("b_p"), st("b_s"), st("c_p"), st("c_s"),
            st("k_p"), st("v_p"), st("ki_p"), st("k_s"), st("v_s"), st("ki_s"),
            jnp.stack(ffn_p), jnp.stack(ffn_s))
```

```python
import functools

import jax
import jax.numpy as jnp
import numpy as np
from jax import lax
from jax.experimental import pallas as pl
from jax.experimental.pallas import tpu as pltpu

D_MODEL = 1024
BATCH = 4
SEQ = 4096
DEPTH = 2
DEC_BATCH = 128
DEC_SEQ = 4
PAST_LEN = 2048
PAGE_SIZE = 128
N_PAGES = PAST_LEN // PAGE_SIZE

A_W = D_MODEL // 2
B_W = D_MODEL // 2
C_W = D_MODEL // 2
ATT_W = D_MODEL // 2
CHUNK = 128
A_GROUPS = A_W // CHUNK
CONV_W = 3
POOL_WINDOWS = (2, 4, 8, 16)
C_GROUPS = len(POOL_WINDOWS)
C_GROUP_W = C_W // C_GROUPS
POOL_HIST = max(POOL_WINDOWS) - 1
HEAD_DIM = 64
N_HEADS = ATT_W // HEAD_DIM
IDX_HEADS = 8
IDX_DIM = 64
TOPK = 256
Q_BLOCK = 128
ROPE_THETA = 500000.0
ROT = HEAD_DIM // 4
D_FF = ((8 * D_MODEL // 3 + 127) // 128) * 128
ALPHA = (2.0 * DEPTH) ** 0.25
LN_EPS = 1e-5
ATT_SCALE = HEAD_DIM ** -0.5
LOG2E = 1.4426950408889634

E_U, E_V, E_B, E_C, E_H = 0, A_W, 2 * A_W, 2 * A_W + B_W, 2 * A_W + 2 * B_W
O_P, O_Q, O_K, O_V, O_QI = 0, C_W, C_W + ATT_W, C_W + 2 * ATT_W, C_W + 3 * ATT_W
O_KI = O_QI + IDX_HEADS * IDX_DIM
O_WI = O_KI + IDX_DIM

LANES = 128
SUBLANES = 8
VMEM_LIMIT = 56 * 1024 * 1024

TM = 512
TM_POST = 256
FF_CHUNK = 256
KEY_CHUNK = 512
SCAN_CHUNK = 256
S_ROWS = DEC_SEQ * DEC_BATCH
INT_MIN = -2 ** 31
NEG_BIG = -1e30

bf16 = jnp.bfloat16
f32 = jnp.float32


def _ln(x, g, b):
    mu = jnp.mean(x, axis=-1, keepdims=True)
    xc = x - mu
    var = jnp.mean(xc * xc, axis=-1, keepdims=True)
    return xc * lax.rsqrt(var + LN_EPS) * g + b


def _gelu(x):
    c = (2.0 / jnp.pi) ** 0.5
    return x * (0.5 * (1.0 + jnp.tanh(c * (x + 0.044715 * (x * x * x)))))


def _dot(a, b):
    return jnp.dot(a, b, preferred_element_type=f32)


def _dot_nt(a, b):
    return lax.dot_general(a, b, (((1,), (1,)), ((), ())), preferred_element_type=f32)


def _prev_rows_prompt(cur, halo):
    ext = jnp.concatenate([halo, cur], axis=0)
    p1 = pltpu.roll(ext, 1, 0)[SUBLANES:]
    p2 = pltpu.roll(ext, 2, 0)[SUBLANES:]
    return p1, p2


def _prev_rows_sample(cur, hist):
    n = DEC_BATCH
    p1 = jnp.concatenate([hist[n:2 * n], cur[0:3 * n]], axis=0)
    p2 = jnp.concatenate([hist, cur[0:2 * n]], axis=0)
    return p1, p2


def _causal_conv(cur, p1, p2, w_ref, c0, width):
    return (cur * w_ref[2:3, c0:c0 + width] + p1 * w_ref[1:2, c0:c0 + width]
            + p2 * w_ref[0:1, c0:c0 + width])


def _even_pre_kernel(*refs, sample):
    if sample:
        (x_ref, w_ref, lg_ref, lb_ref, wcoef_ref, bias_ref, cw_ref, hist_ref,
         a_ref, b_ref, v_ref, e_ref) = refs
    else:
        (x_ref, w_ref, lg_ref, lb_ref, ws_ref, bias_ref, cw_ref,
         a_ref, b_ref, st_ref, halo_ref) = refs

        @pl.when(pl.program_id(1) == 0)
        def _():
            halo_ref[...] = jnp.zeros_like(halo_ref)

    xb = x_ref[...].astype(bf16)
    rows = xb.shape[0]

    def proj(c0):
        return _dot(xb, w_ref[:, c0:c0 + A_W])

    u = _gelu(proj(E_U))
    v = _ln(_gelu(proj(E_V)), lg_ref[...], lb_ref[...])
    if sample:
        v_ref[...] = v
        n = DEC_BATCH
        for t in range(DEC_SEQ):
            mix = bias_ref[t:t + 1, :]
            for s in range(t + 1):
                r = t * DEC_SEQ + s
                mix = mix + wcoef_ref[r:r + 1, :] * v[s * n:(s + 1) * n]
            a_ref[t * n:(t + 1) * n, :] = (u[t * n:(t + 1) * n] * mix).astype(bf16)
    else:
        vb = v.astype(bf16)
        ri = lax.broadcasted_iota(jnp.int32, (CHUNK, CHUNK), 0)
        ci = lax.broadcasted_iota(jnp.int32, (CHUNK, CHUNK), 1)
        for g in range(A_GROUPS):
            gs = slice(g * CHUNK, (g + 1) * CHUNK)
            wc = jnp.where(ri >= ci, ws_ref[g], 0.0).astype(bf16)
            for c in range(rows // CHUNK):
                rs = slice(c * CHUNK, (c + 1) * CHUNK)
                mix = _dot(wc, vb[rs, gs]) + bias_ref[:, gs]
                a_ref[rs, gs] = (u[rs, gs] * mix).astype(bf16)

    e = proj(E_C) * proj(E_H)
    if sample:
        e_ref[...] = e
        p1, p2 = _prev_rows_sample(e, hist_ref[...])
    else:
        p1, p2 = _prev_rows_prompt(e, halo_ref[...])
        halo_ref[...] = e[rows - SUBLANES:]
        st_ref[...] = e[rows - SUBLANES:]
    conv = _causal_conv(e, p1, p2, cw_ref, 0, B_W)
    b_ref[...] = (proj(E_B) * conv).astype(bf16)


def _full(shape):
    nd = len(shape)
    return pl.BlockSpec(shape, lambda *_: (0,) * nd)


def _even_pre_prompt(x, w, lg, lb, ws, bias, cw):
    nt = SEQ // TM
    row = lambda c: pl.BlockSpec((None, TM, c), lambda b, t: (b, t, 0))
    return pl.pallas_call(
        functools.partial(_even_pre_kernel, sample=False),
        grid=(BATCH, nt),
        in_specs=[row(D_MODEL), _full(w.shape), _full(lg.shape), _full(lb.shape), _full(ws.shape),
                  _full(bias.shape), _full(cw.shape)],
        out_specs=[row(A_W), row(B_W), pl.BlockSpec((None, SUBLANES, B_W), lambda b, t: (b, 0, 0))],
        out_shape=[jax.ShapeDtypeStruct((BATCH, SEQ, A_W), bf16),
                   jax.ShapeDtypeStruct((BATCH, SEQ, B_W), bf16),
                   jax.ShapeDtypeStruct((BATCH, SUBLANES, B_W), f32)],
        scratch_shapes=[pltpu.VMEM((SUBLANES, B_W), f32)],
        compiler_params=pltpu.CompilerParams(dimension_semantics=("arbitrary", "arbitrary"),
                                             vmem_limit_bytes=VMEM_LIMIT),
        name="even_pre_prompt",
    )(x, w, lg, lb, ws, bias, cw)


def _even_pre_sample(x, w, lg, lb, wcoef, bias, cw, hist):
    args = (x, w, lg, lb, wcoef, bias, cw, hist)
    return pl.pallas_call(
        functools.partial(_even_pre_kernel, sample=True),
        grid=(1,),
        in_specs=[_full(a.shape) for a in args],
        out_specs=[_full((S_ROWS, A_W)), _full((S_ROWS, B_W)), _full((S_ROWS, A_W)), _full((S_ROWS, B_W))],
        out_shape=[jax.ShapeDtypeStruct((S_ROWS, A_W), bf16),
                   jax.ShapeDtypeStruct((S_ROWS, B_W), bf16),
                   jax.ShapeDtypeStruct((S_ROWS, A_W), f32),
                   jax.ShapeDtypeStruct((S_ROWS, B_W), f32)],
        compiler_params=pltpu.CompilerParams(dimension_semantics=("arbitrary",),
                                             vmem_limit_bytes=VMEM_LIMIT),
        name="even_pre_sample",
    )(*args)


def _post_kernel(*refs, sample):
    if sample:
        (x_ref, ca_ref, cb_ref, wo_ref, g1_ref, b1_ref, wu_ref, cw_ref, cb2_ref, wd_ref, g2_ref, b2_ref,
         hist_ref, o_ref, st_ref, act_ref) = refs
    else:
        (x_ref, ca_ref, cb_ref, wo_ref, g1_ref, b1_ref, wu_ref, cw_ref, cb2_ref, wd_ref, g2_ref, b2_ref,
         o_ref, st_ref, act_ref, halo_ref) = refs

        @pl.when(pl.program_id(1) == 0)
        def _():
            halo_ref[...] = jnp.zeros_like(halo_ref)

    half = ca_ref.shape[-1]
    y = _dot(ca_ref[...], wo_ref[0:half, :]) + _dot(cb_ref[...], wo_ref[half:2 * half, :])
    x1 = _ln(ALPHA * x_ref[...] + y, g1_ref[...], b1_ref[...])
    x1b = x1.astype(bf16)
    rows = x1b.shape[0]

    def up(c0):
        return (_dot(x1b, wu_ref[:, c0:c0 + FF_CHUNK]), _dot(x1b, wu_ref[:, D_FF + c0:D_FF + c0 + FF_CHUNK]))

    def conv_half(hu, c0):
        if sample:
            p1, p2 = _prev_rows_sample(hu, hist_ref[:, c0:c0 + FF_CHUNK])
            st_ref[:, c0:c0 + FF_CHUNK] = hu[2 * DEC_BATCH:]
        else:
            p1, p2 = _prev_rows_prompt(hu, halo_ref[:, c0:c0 + FF_CHUNK])
            halo_ref[:, c0:c0 + FF_CHUNK] = hu[rows - SUBLANES:]
            st_ref[:, c0:c0 + FF_CHUNK] = hu[rows - SUBLANES:]
        return _causal_conv(hu, p1, p2, cw_ref, c0, FF_CHUNK) + cb2_ref[:, c0:c0 + FF_CHUNK]

    n_ff = D_FF // FF_CHUNK
    nxt = up(0)
    for c in range(n_ff):
        c0 = c * FF_CHUNK
        cur = nxt
        if c + 1 < n_ff:
            nxt = up(c0 + FF_CHUNK)
        hg = conv_half(cur[0], c0)
        hv = conv_half(cur[1], D_FF + c0)
        act_ref[:, c0:c0 + FF_CHUNK] = (hg * jax.nn.sigmoid(hg) * hv).astype(bf16)
    o_ref[...] = _ln(ALPHA * x1 + _dot(act_ref[...], wd_ref[...]), g2_ref[...], b2_ref[...])


def _layer_spec(a, layer, **kw):
    return pl.BlockSpec((None,) + a.shape[1:], lambda *_: (layer,) + (0,) * (a.ndim - 1), **kw)


def _post_prompt(layer, x, ca, cb, wo, g1, b1, wu, cw, cb2, wd, g2, b2):
    nt = SEQ // TM_POST
    row = lambda c: pl.BlockSpec((None, TM_POST, c), lambda b, t: (b, t, 0))
    const = lambda a: pl.BlockSpec(a.shape, lambda b, t: (0,) * a.ndim, pipeline_mode=pl.Buffered(1))
    stacked = lambda a: _layer_spec(a, layer, pipeline_mode=pl.Buffered(1))
    return pl.pallas_call(
        functools.partial(_post_kernel, sample=False),
        grid=(BATCH, nt),
        in_specs=[row(D_MODEL), row(ca.shape[-1]), row(cb.shape[-1]), const(wo)]
                 + [stacked(a) for a in (g1, b1, wu, cw, cb2, wd, g2, b2)],
        out_specs=[row(D_MODEL), pl.BlockSpec((None, SUBLANES, 2 * D_FF), lambda b, t: (b, 0, 0))],
        out_shape=[jax.ShapeDtypeStruct((BATCH, SEQ, D_MODEL), f32),
                   jax.ShapeDtypeStruct((BATCH, SUBLANES, 2 * D_FF), f32)],
        scratch_shapes=[pltpu.VMEM((TM_POST, D_FF), bf16), pltpu.VMEM((SUBLANES, 2 * D_FF), f32)],
        compiler_params=pltpu.CompilerParams(dimension_semantics=("arbitrary", "arbitrary"),
                                             vmem_limit_bytes=VMEM_LIMIT,
                                             allow_input_fusion=[i in (3, 6, 9) for i in range(12)]),
        name="post_prompt",
    )(x, ca, cb, wo, g1, b1, wu, cw, cb2, wd, g2, b2)


def _post_sample(layer, x, ca, cb, wo, g1, b1, wu, cw, cb2, wd, g2, b2, hist):
    args = (x, ca, cb, wo, g1, b1, wu, cw, cb2, wd, g2, b2, hist)
    full = lambda a: _full(a.shape)
    stacked = lambda a: _layer_spec(a, layer)
    return pl.pallas_call(
        functools.partial(_post_kernel, sample=True),
        grid=(1,),
        in_specs=[full(x), full(ca), full(cb), full(wo)]
                 + [stacked(a) for a in (g1, b1, wu, cw, cb2, wd, g2, b2)] + [full(hist)],
        out_specs=[_full((S_ROWS, D_MODEL)), _full((2 * DEC_BATCH, 2 * D_FF))],
        out_shape=[jax.ShapeDtypeStruct((S_ROWS, D_MODEL), f32),
                   jax.ShapeDtypeStruct((2 * DEC_BATCH, 2 * D_FF), f32)],
        scratch_shapes=[pltpu.VMEM((S_ROWS, D_FF), bf16)],
        compiler_params=pltpu.CompilerParams(dimension_semantics=("arbitrary",),
                                             vmem_limit_bytes=VMEM_LIMIT),
        name="post_sample",
    )(*args)


def _rope(blk, cos, s1, s2):
    return blk * cos + pltpu.roll(blk, ROT // 2, 1) * s1 + pltpu.roll(blk, LANES - ROT // 2, 1) * s2


def _odd_pre_kernel(*refs, sample):
    if sample:
        (x_ref, wm_ref, wt_ref, cos_ref, s1_ref, s2_ref, cwt_ref, cs_ref, hist_ref,
         k_ref, v_ref, ki_ref, q_ref, qi_ref, wi_ref, co_ref, p_ref) = refs
    else:
        (x_ref, wm_ref, wt_ref, cos_ref, s1_ref, s2_ref, cwt_ref, cs_ref,
         k_ref, v_ref, ki_ref, qb_ref, kb_ref, va_ref, qit_ref, kib_ref, wit_ref, co_ref, ph_ref,
         halo_ref) = refs

        @pl.when(pl.program_id(1) == 0)
        def _():
            halo_ref[...] = jnp.zeros_like(halo_ref)

    xb = x_ref[...].astype(bf16)
    rows = xb.shape[0]
    cos, s1, s2 = cos_ref[...], s1_ref[...], s2_ref[...]

    def proj(c0):
        return _dot(xb, wm_ref[:, c0:c0 + C_W])

    def rope_wide(h):
        return jnp.concatenate(
            [_rope(h[:, j * LANES:(j + 1) * LANES], cos, s1, s2) for j in range(h.shape[1] // LANES)], axis=1)

    q = rope_wide(proj(O_Q))
    k = rope_wide(proj(O_K))
    v = proj(O_V)
    qi = rope_wide(proj(O_QI))
    tail = _dot(xb, wt_ref[...])
    ki2 = _rope(tail[:, 0:LANES], cos, s1, s2)
    k_ref[...] = k
    v_ref[...] = v
    ki_ref[...] = ki2[:, 0:IDX_DIM]
    wi = tail[:, LANES:2 * LANES]
    if sample:
        q_ref[...] = q
        qi_ref[...] = qi
        wi_ref[...] = wi
    else:
        qb_ref[...] = (q * (ATT_SCALE * LOG2E)).astype(bf16)
        kb_ref[...] = k.astype(bf16)
        low = lax.broadcasted_iota(jnp.int32, (1, LANES), 1) < HEAD_DIM
        for j in range(N_HEADS // 2):
            vj = v[:, j * LANES:(j + 1) * LANES]
            va_ref[:, 2 * j * LANES:(2 * j + 1) * LANES] = jnp.where(low, vj, 1.0).astype(bf16)
            va_ref[:, (2 * j + 1) * LANES:(2 * j + 2) * LANES] = jnp.where(low, 1.0, vj).astype(bf16)
        qit_ref[...] = qi.T.astype(bf16)
        kib_ref[...] = ki2[:, 0:IDX_DIM].astype(bf16)
        wit_ref[...] = wi.T[0:IDX_HEADS]

    p = proj(O_P)
    if sample:
        p_ref[...] = p
        n = DEC_BATCH
        hist = hist_ref[...]

        def slab(j, gs):
            if j < POOL_HIST:
                return hist[j * n:(j + 1) * n, gs]
            return p[(j - POOL_HIST) * n:(j - POOL_HIST + 1) * n, gs]

        for g, w in enumerate(POOL_WINDOWS):
            gs = slice(g * C_GROUP_W, (g + 1) * C_GROUP_W)
            for t in range(DEC_SEQ):
                win = slab(POOL_HIST + t, gs)
                for i in range(1, w):
                    win = win + slab(POOL_HIST + t - i, gs)
                pooled = win / float(w) - p[t * n:(t + 1) * n, gs]
                co = _dot(pooled.astype(bf16), cwt_ref[g]) * cs_ref[:, gs]
                co_ref[t * n:(t + 1) * n, gs] = co.astype(bf16)
    else:
        hrows = 2 * SUBLANES
        ext = jnp.concatenate([halo_ref[...], p], axis=0)
        pos = pl.program_id(1) * rows + lax.broadcasted_iota(jnp.int32, (rows, 1), 0)
        for g, w in enumerate(POOL_WINDOWS):
            gs = slice(g * C_GROUP_W, (g + 1) * C_GROUP_W)
            s = ext[:, gs]
            step = 1
            while step < w:
                s = s + pltpu.roll(s, step, 0)
                step *= 2
            cnt = jnp.minimum(w, pos + 1).astype(f32)
            pooled = s[hrows:] / cnt - p[:, gs]
            co = _dot(pooled.astype(bf16), cwt_ref[g]) * cs_ref[:, gs]
            co_ref[:, gs] = co.astype(bf16)
        halo_ref[...] = p[rows - hrows:]
        ph_ref[...] = p[rows - hrows:]


def _odd_pre_prompt(x, wm, wt, cos, s1, s2, cwt, cs):
    nt = SEQ // TM
    row = lambda c: pl.BlockSpec((None, TM, c), lambda b, t: (b, t, 0))
    col = lambda r: pl.BlockSpec((None, r, TM), lambda b, t: (b, 0, t))
    tab = pl.BlockSpec((TM, LANES), lambda b, t: (t, 0))
    sds = lambda c, dt: jax.ShapeDtypeStruct((BATCH, SEQ, c), dt)
    sdt = lambda r, dt: jax.ShapeDtypeStruct((BATCH, r, SEQ), dt)
    hrows = 2 * SUBLANES
    return pl.pallas_call(
        functools.partial(_odd_pre_kernel, sample=False),
        grid=(BATCH, nt),
        in_specs=[row(D_MODEL), _full(wm.shape), _full(wt.shape), tab, tab, tab, _full(cwt.shape),
                  _full(cs.shape)],
        out_specs=[row(ATT_W), row(ATT_W), row(IDX_DIM), row(ATT_W), row(ATT_W), row(2 * ATT_W),
                   col(IDX_HEADS * IDX_DIM), row(IDX_DIM), col(IDX_HEADS), row(C_W),
                   pl.BlockSpec((None, hrows, C_W), lambda b, t: (b, 0, 0))],
        out_shape=[sds(ATT_W, f32), sds(ATT_W, f32), sds(IDX_DIM, f32), sds(ATT_W, bf16), sds(ATT_W, bf16),
                   sds(2 * ATT_W, bf16), sdt(IDX_HEADS * IDX_DIM, bf16), sds(IDX_DIM, bf16), sdt(IDX_HEADS, f32),
                   sds(C_W, bf16), jax.ShapeDtypeStruct((BATCH, hrows, C_W), f32)],
        scratch_shapes=[pltpu.VMEM((hrows, C_W), f32)],
        compiler_params=pltpu.CompilerParams(dimension_semantics=("arbitrary", "arbitrary"),
                                             vmem_limit_bytes=VMEM_LIMIT),
        name="odd_pre_prompt",
    )(x, wm, wt, cos, s1, s2, cwt, cs)


def _odd_pre_sample(x, wm, wt, cos, s1, s2, cwt, cs, hist):
    args = (x, wm, wt, cos, s1, s2, cwt, cs, hist)
    sds = lambda c, dt: jax.ShapeDtypeStruct((S_ROWS, c), dt)
    outs = [sds(ATT_W, f32), sds(ATT_W, f32), sds(IDX_DIM, f32), sds(ATT_W, f32), sds(ATT_W, f32),
            sds(LANES, f32), sds(C_W, bf16), sds(C_W, f32)]
    return pl.pallas_call(
        functools.partial(_odd_pre_kernel, sample=True),
        grid=(1,),
        in_specs=[_full(a.shape) for a in args],
        out_specs=[_full(o.shape) for o in outs],
        out_shape=outs,
        compiler_params=pltpu.CompilerParams(dimension_semantics=("arbitrary",),
                                             vmem_limit_bytes=VMEM_LIMIT),
        name="odd_pre_sample",
    )(*args)


def _order_key(s):
    s = jnp.where(s == 0.0, 0.0, s)
    bits = lax.bitcast_convert_type(s, jnp.int32)
    return bits ^ ((bits >> 31) & 0x7FFFFFFF)


def _dsa_prompt_kernel(q_ref, qit_ref, wit_ref, k_ref, va_ref, ki_ref, o_ref,
                       keys_ref, tpos_ref, m_ref, acc_ref, lga_ref, lgb_ref, da_ref, db_ref):
    qb = pl.program_id(1)
    n_chunks = qb // (KEY_CHUNK // Q_BLOCK) + 1
    qpos = qb * Q_BLOCK + lax.broadcasted_iota(jnp.int32, (1, Q_BLOCK), 1)
    krow = lax.broadcasted_iota(jnp.int32, (KEY_CHUNK, 1), 0)
    kf = float(TOPK)

    qit = qit_ref[...]
    stat = jnp.concatenate([qit[h * IDX_DIM:(h + 1) * IDX_DIM, :] for h in range(IDX_HEADS)], axis=1)
    wit = wit_ref[...]

    def idx_dots(c, d_ref):
        ks = pl.multiple_of(c * KEY_CHUNK, KEY_CHUNK)
        d_ref[...] = _dot(ki_ref[pl.ds(ks, KEY_CHUNK), :], stat)

    def idx_keys(c, d_ref):
        ks = pl.multiple_of(c * KEY_CHUNK, KEY_CHUNK)
        s = jnp.zeros((KEY_CHUNK, Q_BLOCK), f32)
        for h in range(IDX_HEADS):
            s = s + wit[h:h + 1, :] * jnp.maximum(d_ref[:, h * Q_BLOCK:(h + 1) * Q_BLOCK], 0.0)
        keys_ref[pl.ds(ks, KEY_CHUNK), :] = jnp.where((ks + krow) <= qpos, _order_key(s), INT_MIN)

    def two_chunk_pipeline(produce, consume, buf_a, buf_b):
        last = n_chunks - 1
        produce(0, buf_a)

        def body(i, carry):
            produce(2 * i + 1, buf_b)
            consume(2 * i, buf_a)
            produce(jnp.minimum(2 * i + 2, last), buf_a)
            consume(2 * i + 1, buf_b)
            return carry

        lax.fori_loop(0, n_chunks // 2, body, 0)

        @pl.when(lax.rem(n_chunks, 2) == 1)
        def _():
            consume(last, buf_a)

    two_chunk_pipeline(idx_dots, idx_keys, da_ref, db_ref)

    def count(pred, trips, width):
        lanes_acc = 4

        def body(c, accs):
            ks = c * width if isinstance(c, int) else pl.multiple_of(c * width, width)
            m3 = jnp.where(pred(ks, width), 1.0, 0.0).reshape(width // SUBLANES, SUBLANES, LANES)
            accs = list(accs)
            for i in range(width // SUBLANES):
                accs[i % lanes_acc] = accs[i % lanes_acc] + m3[i]
            return tuple(accs)

        accs = (jnp.zeros((SUBLANES, LANES), f32),) * lanes_acc
        if isinstance(trips, int):
            for c in range(trips):
                accs = body(c, accs)
        else:
            accs = lax.fori_loop(0, trips, body, accs)
        return jnp.sum((accs[0] + accs[1]) + (accs[2] + accs[3]), axis=0, keepdims=True)

    def select(n_scan):
        def cnt_ge(cand):
            return count(lambda ks, w: keys_ref[pl.ds(ks, w), :] >= cand, n_scan, SCAN_CHUNK)

        zero = jnp.zeros((1, LANES), jnp.int32)
        c0 = cnt_ge(zero)
        ok0 = c0 >= kf
        t0 = jnp.where(ok0, zero, jnp.full((1, LANES), INT_MIN, jnp.int32))
        ct0 = jnp.where(ok0, c0, jnp.full((1, LANES), 1e9, f32))

        def body(i, st):
            t, ct = st
            cand = t | lax.shift_left(jnp.int32(1), 30 - i)
            c = cnt_ge(cand)
            ok = c >= kf
            return jnp.where(ok, cand, t), jnp.where(ok, c, ct)

        thr, cthr = lax.fori_loop(0, 31, body, (t0, ct0))
        settled = jnp.logical_or(cthr == kf, thr == INT_MIN)

        def tie_search():
            need = kf - count(lambda ks, w: keys_ref[pl.ds(ks, w), :] > thr, n_scan, SCAN_CHUNK)
            far = jnp.int32(2 * SEQ)
            for c in range(n_scan):
                rs = pl.ds(c * SCAN_CHUNK, SCAN_CHUNK)
                pos = c * SCAN_CHUNK + lax.broadcasted_iota(jnp.int32, (SCAN_CHUNK, LANES), 0)
                tpos_ref[rs, :] = jnp.where(keys_ref[rs, :] == thr, pos, far)
            bits = (n_scan * SCAN_CHUNK - 1).bit_length() + 1

            def jbit(i, j):
                cand = j | lax.shift_left(jnp.int32(1), bits - 1 - i)
                c = count(lambda ks, w: tpos_ref[pl.ds(ks, w), :] < cand, n_scan, SCAN_CHUNK)
                return jnp.where(c <= need, cand, j)

            return lax.fori_loop(0, bits, jbit, zero)

        has_tie = jnp.max(jnp.where(settled, 0.0, 1.0)) > 0.0
        jsel = lax.cond(has_tie, tie_search, lambda: jnp.full((1, LANES), 2 * SEQ, jnp.int32))
        return jnp.maximum(thr, INT_MIN + 1), jsel

    n_scan = (qb * Q_BLOCK + Q_BLOCK + SCAN_CHUNK - 1) // SCAN_CHUNK
    thr, jsel = lax.switch(
        n_scan - 1, [functools.partial(select, k) for k in range(1, SEQ // SCAN_CHUNK + 1)])

    low = lax.broadcasted_iota(jnp.int32, (1, LANES), 1) < HEAD_DIM
    pairs = N_HEADS // 2
    m_ref[...] = jnp.full(m_ref.shape, NEG_BIG, f32)
    acc_ref[...] = jnp.zeros_like(acc_ref)
    q = q_ref[...]
    eye = (lax.broadcasted_iota(jnp.int32, (Q_BLOCK, Q_BLOCK), 0)
           == lax.broadcasted_iota(jnp.int32, (Q_BLOCK, Q_BLOCK), 1)).astype(bf16)
    qpair = []
    for j in range(pairs):
        blk = q[:, j * LANES:(j + 1) * LANES]
        nil = jnp.zeros_like(blk)
        rows2 = jnp.concatenate([jnp.where(low, blk, nil), jnp.where(low, nil, blk)], axis=0)
        qpair.append(jnp.concatenate([rows2, jnp.concatenate([eye, eye], axis=0)], axis=1))

    def qk_all(c, lg_ref):
        ks = pl.multiple_of(c * KEY_CHUNK, KEY_CHUNK)
        kk = keys_ref[pl.ds(ks, KEY_CHUNK), :]
        pos = ks + lax.broadcasted_iota(jnp.int32, (KEY_CHUNK, LANES), 0)
        b = jnp.where(kk > thr, 0.0, jnp.where(kk == thr, jnp.where(pos < jsel, 0.0, NEG_BIG), NEG_BIG))
        bias_t = b.astype(bf16)
        for j in range(pairs):
            rhs = jnp.concatenate([k_ref[pl.ds(ks, KEY_CHUNK), j * LANES:(j + 1) * LANES], bias_t], axis=1)
            lg_ref[j] = _dot_nt(qpair[j], rhs)

    def soft_pv(c, lg_ref):
        ks = pl.multiple_of(c * KEY_CHUNK, KEY_CHUNK)
        for j in range(pairs):
            lg = lg_ref[j]
            m_old = m_ref[j]
            m_new = jnp.maximum(m_old, jnp.max(lg, axis=1, keepdims=True))
            p = jnp.exp2(lg - m_new).astype(bf16)
            a = jnp.exp2(m_old - m_new)
            pv = _dot(p, va_ref[pl.ds(ks, KEY_CHUNK), 2 * j * LANES:(2 * j + 2) * LANES])
            own = jnp.concatenate([pv[0:Q_BLOCK, 0:LANES], pv[Q_BLOCK:2 * Q_BLOCK, LANES:2 * LANES]], axis=0)
            acc_ref[j] = a * acc_ref[j] + own
            m_ref[j] = m_new

    two_chunk_pipeline(qk_all, soft_pv, lga_ref, lgb_ref)

    for j in range(pairs):
        acc = acc_ref[j]
        out = acc / pltpu.roll(acc, HEAD_DIM, 1)
        o_ref[:, j * LANES:(j + 1) * LANES] = jnp.where(low, out[0:Q_BLOCK], out[Q_BLOCK:2 * Q_BLOCK]).astype(bf16)


def _dsa_prompt(qb, qit, wit, kb, va, kib):
    nq = SEQ // Q_BLOCK
    qrow = lambda c: pl.BlockSpec((None, Q_BLOCK, c), lambda b, t: (b, t, 0))
    qcol = lambda r: pl.BlockSpec((None, r, Q_BLOCK), lambda b, t: (b, 0, t))
    seq = lambda c: pl.BlockSpec((None, SEQ, c), lambda b, t: (b, 0, 0))
    return pl.pallas_call(
        _dsa_prompt_kernel,
        grid=(BATCH, nq),
        in_specs=[qrow(ATT_W), qcol(IDX_HEADS * IDX_DIM), qcol(IDX_HEADS), seq(ATT_W), seq(2 * ATT_W),
                  seq(IDX_DIM)],
        out_specs=qrow(ATT_W),
        out_shape=jax.ShapeDtypeStruct((BATCH, SEQ, ATT_W), bf16),
        scratch_shapes=[pltpu.VMEM((SEQ, Q_BLOCK), jnp.int32),
                        pltpu.VMEM((SEQ, Q_BLOCK), jnp.int32),
                        pltpu.VMEM((N_HEADS // 2, 2 * Q_BLOCK, 1), f32),
                        pltpu.VMEM((N_HEADS // 2, 2 * Q_BLOCK, LANES), f32),
                        pltpu.VMEM((N_HEADS // 2, 2 * Q_BLOCK, KEY_CHUNK), f32),
                        pltpu.VMEM((N_HEADS // 2, 2 * Q_BLOCK, KEY_CHUNK), f32),
                        pltpu.VMEM((KEY_CHUNK, IDX_HEADS * Q_BLOCK), f32),
                        pltpu.VMEM((KEY_CHUNK, IDX_HEADS * Q_BLOCK), f32)],
        compiler_params=pltpu.CompilerParams(dimension_semantics=("arbitrary", "arbitrary"),
                                             vmem_limit_bytes=VMEM_LIMIT),
        name="dsa_prompt",
    )(qb, qit, wit, kb, va, kib)


S_KEYS = PAST_LEN + PAGE_SIZE
BITS_PER_PASS = 4
SEQ_PER_STEP = 2


def _dsa_sample_kernel(pt_ref, q_ref, qi_ref, wi_ref, kin_ref, kn_ref, vn_ref, *rest):
    g_n = SEQ_PER_STEP
    kidx_refs = rest[0:g_n * N_PAGES]
    k_refs = rest[g_n * N_PAGES:2 * g_n * N_PAGES]
    v_refs = rest[2 * g_n * N_PAGES:3 * g_n * N_PAGES]
    o_ref = rest[3 * g_n * N_PAGES]
    del pt_ref
    rows = DEC_SEQ * IDX_HEADS
    kf = float(TOPK)
    kpos = lax.broadcasted_iota(jnp.int32, (1, S_KEYS), 1)
    tq = lax.broadcasted_iota(jnp.int32, (DEC_SEQ, 1), 0)

    def pad_keys(x):
        return jnp.concatenate([x, jnp.zeros((PAGE_SIZE - x.shape[0], x.shape[1]), x.dtype)], axis=0)

    def count(pred):
        return jnp.sum(jnp.where(pred, 1.0, 0.0), axis=1, keepdims=True)

    def order_keys(g):
        qi = qi_ref[g].astype(bf16)
        wi = wi_ref[g]

        def idx_score(d):
            r = jnp.maximum(d, 0.0) * wi
            return jnp.sum(r.reshape(DEC_SEQ, IDX_HEADS, PAGE_SIZE), axis=1)

        parts = [idx_score(_dot(qi, kidx_refs[g * N_PAGES + j][...].astype(bf16))) for j in range(N_PAGES)]
        parts.append(idx_score(_dot_nt(qi, pad_keys(kin_ref[g]).astype(bf16))))
        s = jnp.concatenate(parts, axis=1)
        return jnp.where((kpos - PAST_LEN) <= tq, _order_key(s), INT_MIN)

    hrow = lax.broadcasted_iota(jnp.int32, (rows, ATT_W), 0) % N_HEADS
    hcol = lax.broadcasted_iota(jnp.int32, (rows, ATT_W), 1) // HEAD_DIM
    own = hrow == hcol

    def raw_logits(g):
        q4 = q_ref[g] * ATT_SCALE
        q32 = jnp.broadcast_to(q4[:, None, :], (DEC_SEQ, N_HEADS, ATT_W)).reshape(rows, ATT_W)
        qbd = jnp.where(own, q32, 0.0).astype(bf16)
        kn = pad_keys(kn_ref[g]).astype(bf16)
        return jnp.concatenate([_dot(qbd, k_refs[g * N_PAGES + j][...].astype(bf16)) for j in range(N_PAGES)]
                               + [_dot_nt(qbd, kn)], axis=1)

    lg_raw = [raw_logits(g) for g in range(g_n)]

    keys = jnp.concatenate([order_keys(g) for g in range(g_n)], axis=0)
    sel_rows = g_n * DEC_SEQ
    zero = jnp.zeros((sel_rows, 1), jnp.int32)
    c0 = count(keys >= zero)
    thr = jnp.where(c0 >= kf, zero, jnp.full((sel_rows, 1), INT_MIN, jnp.int32))
    cthr = jnp.where(c0 >= kf, c0, 1e9)
    shift = 31
    while shift > 0:
        nb = min(BITS_PER_PASS, shift)
        shift -= nb
        num = zero
        for c in range(1, 2 ** nb):
            cnt = count(keys >= (thr | (c << shift)))
            ok = cnt >= kf
            num = num + ok.astype(jnp.int32)
            cthr = jnp.where(ok, cnt, cthr)
        thr = thr | lax.shift_left(num, shift)
    need = kf - count(keys > thr)
    settled = jnp.logical_or(cthr == kf, thr == INT_MIN)

    def tie_search():
        def jbit(i, j):
            cand = j | lax.shift_left(jnp.int32(1), 12 - i)
            c = count(jnp.logical_and(keys == thr, kpos < cand))
            return jnp.where(c <= need, cand, j)
        return lax.fori_loop(0, 13, jbit, zero)

    jsel = lax.cond(jnp.max(jnp.where(settled, 0.0, 1.0)) > 0.0, tie_search,
                    lambda: jnp.full((sel_rows, 1), 2 * S_KEYS, jnp.int32))
    sel_all = jnp.logical_and(
        jnp.logical_or(keys > thr, jnp.logical_and(keys == thr, kpos < jsel)), keys != INT_MIN)
    bias_all = jnp.where(sel_all, 0.0, NEG_BIG)

    for g in range(g_n):
        bias4 = bias_all[g * DEC_SEQ:(g + 1) * DEC_SEQ]
        bias = jnp.broadcast_to(bias4[:, None, :], (DEC_SEQ, N_HEADS, S_KEYS)).reshape(rows, S_KEYS)
        lg = lg_raw[g] + bias
        m = jnp.max(lg, axis=1, keepdims=True)
        p = jnp.exp(lg - m)
        l = jnp.sum(p, axis=1, keepdims=True)
        pb = p.astype(bf16)
        pv = _dot(pb[:, PAST_LEN:], pad_keys(vn_ref[g]).astype(bf16))
        for j in range(N_PAGES):
            pv = pv + _dot_nt(pb[:, j * PAGE_SIZE:(j + 1) * PAGE_SIZE], v_refs[g * N_PAGES + j][...].astype(bf16))
        out = jnp.where(own, pv, 0.0) / l
        o_ref[g] = jnp.sum(out.reshape(DEC_SEQ, N_HEADS, ATT_W), axis=1)


def _dsa_sample(page_table, q, qi, wi, ki_new, k_new, v_new, kidx_pages, k_pages, v_pages, layer):
    n_pool = kidx_pages.shape[0] // (DEPTH // 2)
    rows = DEC_SEQ * IDX_HEADS
    g_n = SEQ_PER_STEP
    per_seq = lambda r, c: pl.BlockSpec((g_n, r, c), lambda s, pt: (s, 0, 0))

    def page(r, g, j):
        return pl.BlockSpec((None, r, PAGE_SIZE), lambda s, pt: (layer * n_pool + pt[s * g_n + g, j], 0, 0))

    pages = lambda r: [page(r, g, j) for g in range(g_n) for j in range(N_PAGES)]
    in_specs = [per_seq(DEC_SEQ, ATT_W), per_seq(rows, IDX_DIM), per_seq(rows, 1),
                per_seq(SUBLANES, IDX_DIM), per_seq(SUBLANES, ATT_W), per_seq(SUBLANES, ATT_W)]
    in_specs += pages(IDX_DIM) + pages(ATT_W) + pages(ATT_W)
    grid_spec = pltpu.PrefetchScalarGridSpec(
        num_scalar_prefetch=1, grid=(DEC_BATCH // g_n,), in_specs=in_specs,
        out_specs=per_seq(DEC_SEQ, ATT_W))
    n_in = g_n * N_PAGES
    return pl.pallas_call(
        _dsa_sample_kernel,
        grid_spec=grid_spec,
        out_shape=jax.ShapeDtypeStruct((DEC_BATCH, DEC_SEQ, ATT_W), f32),
        compiler_params=pltpu.CompilerParams(dimension_semantics=("arbitrary",),
                                             vmem_limit_bytes=VMEM_LIMIT),
        name="dsa_sample",
    )(page_table, q, qi, wi, ki_new, k_new, v_new,
      *([kidx_pages] * n_in), *([k_pages] * n_in), *([v_pages] * n_in))


def _rope_tables(pos):
    half = ROT // 2
    inv = np.power(np.float32(ROPE_THETA), -np.arange(half, dtype=np.float32) * np.float32(2.0) / np.float32(ROT))
    ang = np.asarray(pos, np.float32)[:, None] * inv.astype(np.float32)[None, :]
    cos, sin = np.cos(ang).astype(np.float32), np.sin(ang).astype(np.float32)
    t = len(pos)
    rest = HEAD_DIM - ROT
    c64 = np.concatenate([cos, cos, np.ones((t, rest), np.float32)], axis=1)
    s1 = np.concatenate([np.zeros((t, half), np.float32), sin, np.zeros((t, rest), np.float32)], axis=1)
    s2 = np.concatenate([-sin, np.zeros((t, half + rest), np.float32)], axis=1)
    two = lambda a: jnp.asarray(np.concatenate([a, a], axis=1))
    return two(c64), two(s1), two(s2)


def _to_tm(a):
    return a.transpose(1, 0, 2).reshape(a.shape[1] * a.shape[0], a.shape[2])


def _from_tm(a, t=DEC_SEQ):
    return a.reshape(t, DEC_BATCH, a.shape[-1]).transpose(1, 0, 2)


def kernel(x_prompt, x_sample, state_b_conv, state_c_pool, cache_k, cache_v, cache_kidx, state_ffn_conv, page_table, w_in_even, a_ln_g, a_ln_b, a_ws, a_bs, b_conv_w, w_out_even, w_in_odd, c_w, c_scale, w_out_odd, ln_mix_g, ln_mix_b, ffn_w_up, ffn_conv_w, ffn_conv_b, ffn_w_down, ln_ffn_g, ln_ffn_b):
    row = lambda a: a.reshape(1, -1)
    xp = x_prompt
    xs = _to_tm(x_sample)
    outs = {}
    ffn_p, ffn_s = [], []

    rows3 = lambda a: a[:, None, :]
    ffn_wts = (rows3(ln_mix_g), rows3(ln_mix_b), ffn_w_up.astype(bf16), ffn_conv_w, rows3(ffn_conv_b),
               ffn_w_down.astype(bf16), rows3(ln_ffn_g), rows3(ln_ffn_b))

    def post(layer, xp, xs, cap, cbp, cas, cbs, wo):
        wob = wo.astype(bf16)
        xp, st_p = _post_prompt(layer, xp, cap, cbp, wob, *ffn_wts)
        xs, st_s = _post_sample(layer, xs, cas, cbs, wob, *ffn_wts, _to_tm(state_ffn_conv[layer]))
        ffn_p.append(st_p[:, SUBLANES - (CONV_W - 1):])
        ffn_s.append(_from_tm(st_s, CONV_W - 1))
        return xp, xs

    for layer in range(DEPTH):
        if layer % 2 == 0:
            e = layer // 2
            w = w_in_even[e].astype(bf16)
            lg, lb = row(a_ln_g[e]), row(a_ln_b[e])
            bias = jnp.repeat(a_bs[e].T, CHUNK, axis=1)
            wcoef = jnp.repeat(a_ws[e][:, :DEC_SEQ, :DEC_SEQ].transpose(1, 2, 0).reshape(DEC_SEQ * DEC_SEQ, A_GROUPS),
                               CHUNK, axis=1)
            a_p, b_p, bst_p = _even_pre_prompt(xp, w, lg, lb, a_ws[e], bias, b_conv_w[e])
            a_s, b_s, v_s, e_s = _even_pre_sample(xs, w, lg, lb, wcoef, bias, b_conv_w[e],
                                                  _to_tm(state_b_conv[e]))
            outs.setdefault("a_v_s", []).append(_from_tm(v_s))
            outs.setdefault("b_p", []).append(bst_p[:, SUBLANES - (CONV_W - 1):])
            outs.setdefault("b_s", []).append(_from_tm(e_s)[:, DEC_SEQ - (CONV_W - 1):])
            xp, xs = post(layer, xp, xs, a_p, b_p, a_s, b_s, w_out_even[e])
        else:
            o = layer // 2
            wm = w_in_odd[o][:, :O_KI].astype(bf16)
            wki = w_in_odd[o][:, O_KI:O_WI]
            wwi = w_in_odd[o][:, O_WI:]
            wt = jnp.concatenate([wki, wki, wwi, jnp.zeros((D_MODEL, LANES - IDX_HEADS), f32)],
                                 axis=1).astype(bf16)
            cwt = c_w[o].astype(bf16)
            cs = row(c_scale[o])
            tabs_p = _rope_tables(np.arange(SEQ))
            tabs_s = _rope_tables(PAST_LEN + np.repeat(np.arange(DEC_SEQ), DEC_BATCH))
            (k_p, v_p, ki_p, qb, kb, va, qit, kib, wit, co_p, ph_p) = _odd_pre_prompt(
                xp, wm, wt, *tabs_p, cwt, cs)
            att_p = _dsa_prompt(qb, qit, wit, kb, va, kib)
            (k_s, v_s2, ki_s, q_s, qi_s, wi_s, co_s, p_s) = _odd_pre_sample(
                xs, wm, wt, *tabs_s, cwt, cs, _to_tm(state_c_pool[o]))
            rows = DEC_SEQ * IDX_HEADS
            pad8 = lambda a: jnp.pad(_from_tm(a), ((0, 0), (0, SUBLANES - DEC_SEQ), (0, 0)))
            page_t = lambda c, w: jnp.moveaxis(c, 2, -1).reshape(-1, w, PAGE_SIZE)
            att_s = _dsa_sample(
                page_table,
                _from_tm(q_s),
                _from_tm(qi_s).reshape(DEC_BATCH, rows, IDX_DIM),
                _from_tm(wi_s)[:, :, :IDX_HEADS].reshape(DEC_BATCH, rows, 1),
                pad8(ki_s), pad8(k_s), pad8(v_s2),
                page_t(cache_kidx, IDX_DIM), page_t(cache_k, ATT_W), page_t(cache_v, ATT_W), o)
            heads = lambda a: a.reshape(a.shape[0], a.shape[1], N_HEADS, HEAD_DIM)
            outs.setdefault("c_p", []).append(ph_p[:, 2 * SUBLANES - POOL_HIST:])
            outs.setdefault("c_s", []).append(
                jnp.concatenate([state_c_pool[o][:, DEC_SEQ:], _from_tm(p_s)], axis=1))
            outs.setdefault("k_p", []).append(heads(k_p))
            outs.setdefault("v_p", []).append(heads(v_p))
            outs.setdefault("ki_p", []).append(ki_p)
            outs.setdefault("k_s", []).append(heads(_from_tm(k_s)))
            outs.setdefault("v_s", []).append(heads(_from_tm(v_s2)))
            outs.setdefault("ki_s", []).append(_from_tm(ki_s))
            xp, xs = post(layer, xp, xs, co_p, att_p, co_s, _to_tm(att_s).astype(bf16), w_out_odd[o])

    st = lambda name: jnp.stack(outs[name])
    return (xp, _from_tm(xs), st("a_v_s"), st("b_p"), st("b_s"), st("c_p"), st("c_s"),
            st("k_p"), st("v_p"), st("ki_p"), st("k_s"), st("v_s"), st("ki_s"),
            jnp.stack(ffn_p), jnp.stack(ffn_s))
```
